```python
import math, functools
import jax, jax.numpy as jnp
from jax import lax
import numpy as np

D_MODEL = 1024
BATCH = 4
SEQ = 4096
DEPTH = 4
DEC_BATCH = 128
DEC_SEQ = 4
PAST_LEN = 2048
PAGE_SIZE = 128

HEAD_DIM = 64
WA = D_MODEL // 4
WB = D_MODEL // 2
WC = D_MODEL - WA - WB
HA = WA // HEAD_DIM
HB = WB // HEAD_DIM
HC = WC // HEAD_DIM
DA = HEAD_DIM
DB = HEAD_DIM
DKC = HEAD_DIM
DVC = HEAD_DIM
KC = HC * DKC
CONV_W = 4
CHUNK = 64
MOBA_BLOCK = 256
MOBA_TOPK = 3
Q_BLOCK = 128
D_FF = 2816
N_EXPERTS = 8
TOP_K = 2
ALPHA = (2.0 * DEPTH) ** 0.25
BETA_INIT = (8.0 * DEPTH) ** -0.25
LN_EPS = 1e-5
RMS_EPS = 1e-6

OFF_A_Z = 3 * WA
OFF_A_A = 4 * WA
OFF_A_B = 4 * WA + HA
OFF_B = 4 * WA + 2 * HA
OFF_C = OFF_B + 3 * WB
P_IN = OFF_C + 2 * KC + 2 * WC

kernel_name = 'hybrid_deltanet_moba_hgrn2_decode_step'


def layer_norm(x, g, b):
    xf = x.astype(jnp.float32)
    mu = jnp.mean(xf, -1, keepdims=True)
    var = jnp.mean(jnp.square(xf - mu), -1, keepdims=True)
    return ((xf - mu) * lax.rsqrt(var + LN_EPS) * g + b).astype(x.dtype)


def l2_normalize(x):
    return x * lax.rsqrt(jnp.sum(x * x, -1, keepdims=True) + 1e-6)


def gated_rms_norm(o, w, z):
    o = o * lax.rsqrt(jnp.mean(o * o, -1, keepdims=True) + RMS_EPS)
    return o * w.astype(jnp.float32) * jax.nn.silu(z)


def alibi_slopes(n):
    return jnp.exp2(-8.0 * jnp.arange(1, n + 1, dtype=jnp.float32) / n)


def causal_conv(x, buf, w):
    xp = jnp.concatenate([buf.astype(jnp.float32), x.astype(jnp.float32)], 1)
    y = lax.conv_general_dilated(xp, w.astype(jnp.float32)[:, None, :], (1,), 'VALID',
                                 dimension_numbers=('NWC', 'WIO', 'NWC'),
                                 feature_group_count=x.shape[-1])
    return y, xp[:, xp.shape[1] - (CONV_W - 1):]


def to_chunks(a, n, c):
    b, _, h = a.shape[:3]
    a = a.reshape((b, n, c, h) + a.shape[3:])
    return a.transpose((1, 0, 3, 2) + tuple(range(4, a.ndim)))


def from_chunks(o):
    n, b, h, c, d = o.shape
    return o.transpose(1, 0, 3, 2, 4).reshape(b, n * c, h, d)


def gated_delta_rule(q, k, v, g, beta, s0):
    t = q.shape[1]
    c = math.gcd(t, CHUNK)
    n = t // c
    incl = jnp.tril(jnp.ones((c, c), bool))
    strict = jnp.tril(jnp.ones((c, c), bool), -1)
    eye = jnp.eye(c, dtype=jnp.float32)
    dv = v.shape[-1]

    def step(s, xs):
        qc, kc, vc, gc, bc = xs
        gam = jnp.cumsum(gc, -1)
        decay = jnp.exp(jnp.where(incl, gam[..., :, None] - gam[..., None, :], -jnp.inf))
        kbeta = kc * bc[..., None]
        a = jnp.where(strict, jnp.einsum('bhik,bhjk->bhij', kbeta, kc) * decay, 0.0) + eye
        rhs = jnp.concatenate([vc * bc[..., None], kbeta * jnp.exp(gam)[..., None]], -1)
        sol = lax.linalg.triangular_solve(a, rhs, left_side=True, lower=True)
        u = sol[..., :dv] - jnp.einsum('bhck,bhkv->bhcv', sol[..., dv:], s)
        attn = jnp.einsum('bhik,bhjk->bhij', qc, kc) * decay
        o = (jnp.einsum('bhck,bhkv->bhcv', qc * jnp.exp(gam)[..., None], s)
             + jnp.einsum('bhij,bhjv->bhiv', attn, u))
        glast = gam[..., -1:]
        s = (s * jnp.exp(glast)[..., None]
             + jnp.einsum('bhck,bhcv->bhkv', kc * jnp.exp(glast - gam)[..., None], u))
        return s, o

    xs = (to_chunks(q, n, c), to_chunks(k, n, c), to_chunks(v, n, c),
          to_chunks(g, n, c), to_chunks(beta, n, c))
    s, o = lax.scan(step, s0, xs)
    return from_chunks(o), s


def hgrn2_recurrence(q, k, v, logf, s0):
    t = q.shape[1]
    c = math.gcd(t, CHUNK)
    n = t // c
    incl = jnp.tril(jnp.ones((c, c), bool))[:, :, None]

    def step(s, xs):
        qc, kc, vc, gc = xs
        gam = jnp.cumsum(gc, axis=2)
        w = jnp.exp(jnp.where(incl, gam[:, :, :, None, :] - gam[:, :, None, :, :], -jnp.inf))
        attn = jnp.einsum('bhik,bhjk,bhijk->bhij', qc, kc, w)
        o = (jnp.einsum('bhck,bhkv->bhcv', qc * jnp.exp(gam), s)
             + jnp.einsum('bhij,bhjv->bhiv', attn, vc))
        glast = gam[:, :, -1:, :]
        s = (s * jnp.exp(glast[:, :, 0, :, None])
             + jnp.einsum('bhck,bhcv->bhkv', kc * jnp.exp(glast - gam), vc))
        return s, o

    xs = (to_chunks(q, n, c), to_chunks(k, n, c), to_chunks(v, n, c), to_chunks(logf, n, c))
    s, o = lax.scan(step, s0, xs)
    return from_chunks(o), s


def moba_attention(q, k_all, v_all, q_pos, slopes):
    b, l, h, d = k_all.shape
    nb = -(-l // MOBA_BLOCK)
    pad = nb * MOBA_BLOCK - l

    def blocks(a):
        a = jnp.pad(a.astype(jnp.float32), ((0, 0), (0, pad), (0, 0), (0, 0)))
        return a.reshape(b, nb, MOBA_BLOCK, h, d).transpose(0, 3, 1, 2, 4)

    kb = blocks(k_all)
    vb = blocks(v_all)
    kmean = jnp.mean(kb, axis=3)
    k_sel = min(MOBA_TOPK, nb)
    scale = d ** -0.5
    bi = jnp.arange(b)[:, None, None]
    hi = jnp.arange(h)[None, :, None]
    offs = jnp.arange(MOBA_BLOCK, dtype=jnp.int32)

    def attend(qc, pc):
        tc = qc.shape[1]
        qh = qc.astype(jnp.float32).transpose(0, 2, 1, 3)
        qblk = pc // MOBA_BLOCK
        gate = jnp.einsum('bhqd,bhnd->bhqn', qh, kmean)
        eligible = jnp.arange(nb)[None, :] < qblk[:, None]
        gate = jnp.where(eligible, gate, -jnp.inf)
        top_val, top_idx = lax.top_k(gate, k_sel)
        own = jnp.broadcast_to(qblk[None, None, :, None], (b, h, tc, 1)).astype(top_idx.dtype)
        idx = jnp.concatenate([top_idx, own], -1)
        valid = jnp.concatenate([jnp.isfinite(top_val), jnp.ones((b, h, tc, 1), bool)], -1)
        scores = []
        for s in range(k_sel + 1):
            ks = kb[bi, hi, idx[..., s]]
            sc = jnp.einsum('bhqd,bhqkd->bhqk', qh, ks) * scale
            dist = pc[:, None] - (idx[..., s, None] * MOBA_BLOCK + offs)
            ok = valid[..., s, None] & (dist >= 0)
            scores.append(jnp.where(ok, sc - slopes[:, None, None] * dist.astype(jnp.float32), -jnp.inf))
        p = jax.nn.softmax(jnp.concatenate(scores, -1), axis=-1)
        out = jnp.zeros((b, h, tc, d), jnp.float32)
        for s in range(k_sel + 1):
            vs = vb[bi, hi, idx[..., s]]
            out = out + jnp.einsum('bhqk,bhqkd->bhqd', p[..., s * MOBA_BLOCK:(s + 1) * MOBA_BLOCK], vs)
        return out.transpose(0, 2, 1, 3)

    tq = q.shape[1]
    if tq > Q_BLOCK and tq % Q_BLOCK == 0:
        n = tq // Q_BLOCK
        qs = q.reshape(b, n, Q_BLOCK, h, d).transpose(1, 0, 2, 3, 4)
        ps = q_pos.reshape(n, Q_BLOCK)
        out = lax.map(lambda a: attend(a[0], a[1]), (qs, ps))
        return out.transpose(1, 0, 2, 3, 4).reshape(b, tq, h, d)
    return attend(q, q_pos)


def token_mixers(h, q_pos, k_past, v_past, sa0, conv0, sc0, a_conv_w, a_a_log, a_dt_bias,
                 a_norm_w, lb, c_norm_w, slopes):
    f32 = jnp.float32
    b, t, _ = h.shape
    qkv, conv_new = causal_conv(h[..., :3 * WA], conv0, a_conv_w)
    qkv = jax.nn.silu(qkv)
    qa = l2_normalize(qkv[..., :WA].reshape(b, t, HA, DA)) * DA ** -0.5
    ka = l2_normalize(qkv[..., WA:2 * WA].reshape(b, t, HA, DA))
    va = qkv[..., 2 * WA:].reshape(b, t, HA, DA)
    za = h[..., OFF_A_Z:OFF_A_Z + WA].astype(f32).reshape(b, t, HA, DA)
    dt = jax.nn.softplus(h[..., OFF_A_A:OFF_A_A + HA].astype(f32) + a_dt_bias.astype(f32))
    ga = -jnp.exp(a_a_log.astype(f32)) * dt
    beta = jax.nn.sigmoid(h[..., OFF_A_B:OFF_A_B + HA].astype(f32))
    oa, sa_new = gated_delta_rule(qa, ka, va, ga, beta, sa0.astype(f32))
    oa = gated_rms_norm(oa, a_norm_w, za).reshape(b, t, WA)
    hb = h[..., OFF_B:OFF_B + 3 * WB]
    qb = hb[..., :WB].reshape(b, t, HB, DB)
    kb_new = hb[..., WB:2 * WB].reshape(b, t, HB, DB)
    vb_new = hb[..., 2 * WB:].reshape(b, t, HB, DB)
    if k_past is None:
        k_all, v_all = kb_new, vb_new
    else:
        k_all = jnp.concatenate([k_past.astype(f32), kb_new.astype(f32)], 1)
        v_all = jnp.concatenate([v_past.astype(f32), vb_new.astype(f32)], 1)
    ob = moba_attention(qb, k_all, v_all, q_pos, slopes).reshape(b, t, WB)
    hc = h[..., OFF_C:].astype(f32)
    qc = jax.nn.silu(hc[..., :KC]).reshape(b, t, HC, DKC)
    fx = hc[..., KC:2 * KC].reshape(b, t, HC, DKC)
    ic = hc[..., 2 * KC:2 * KC + WC].reshape(b, t, HC, DVC)
    zc = hc[..., 2 * KC + WC:].reshape(b, t, HC, DVC)
    lb = lb.astype(f32).reshape(HC, DKC)
    log_f = jnp.logaddexp(jnp.log(lb), jnp.log1p(-lb) + jax.nn.log_sigmoid(fx))
    kc = (1.0 - lb) * jax.nn.sigmoid(-fx)
    oc, sc_new = hgrn2_recurrence(qc, kc, ic, log_f, sc0.astype(f32))
    oc = gated_rms_norm(oc, c_norm_w, zc).reshape(b, t, WC)
    mix = jnp.concatenate([oa, ob, oc], -1)
    return mix, (kb_new, vb_new, sa_new, conv_new, sc_new)


def swiglu(x, w1, w3, w2):
    hid = jax.nn.silu(jnp.einsum('btd,df->btf', x, w1)) * jnp.einsum('btd,df->btf', x, w3)
    return jnp.einsum('btf,fd->btd', hid, w2)


def moe_ffn(x, router, w1, w3, w2):
    logits = jnp.einsum('btd,de->bte', x, router).astype(jnp.float32)
    top_val, top_idx = lax.top_k(logits, TOP_K)
    probs = jax.nn.softmax(top_val, axis=-1)
    gate = jnp.sum(jax.nn.one_hot(top_idx, N_EXPERTS, dtype=jnp.float32) * probs[..., None], axis=-2)
    gate = gate.astype(x.dtype)
    y = jnp.zeros_like(x)
    for e in range(N_EXPERTS):
        y = y + gate[..., e:e + 1] * swiglu(x, w1[e], w3[e], w2[e])
    return y


def decoder_layer(x, q_pos, k_past, v_past, sa0, conv0, sc0, w_in, w_out, a_conv_w, a_a_log,
                  a_dt_bias, a_norm_w, lb, c_norm_w, ln1_g, ln1_b, ln2_g, ln2_b, ffn, slopes):
    h = jnp.einsum('btd,dp->btp', x, w_in)
    mix, new_state = token_mixers(h, q_pos, k_past, v_past, sa0, conv0, sc0, a_conv_w, a_a_log,
                                  a_dt_bias, a_norm_w, lb, c_norm_w, slopes)
    x = layer_norm(ALPHA * x + jnp.einsum('btm,md->btd', mix.astype(x.dtype), w_out), ln1_g, ln1_b)
    x = layer_norm(ALPHA * x + ffn(x), ln2_g, ln2_b)
    return x, new_state


def setup_inputs(seed: int = 0) -> dict:
    f32 = jnp.float32
    key = jax.random.key(seed)
    ks = jax.random.split(key, 40)

    def nrm(i, shape, scale):
        return jax.random.normal(ks[i], shape, f32) * scale

    n_pages = PAST_LEN // PAGE_SIZE
    n_used = DEC_BATCH * n_pages
    n_pool = n_used + max(1, n_used // 4)
    n_dense = (DEPTH + 1) // 2
    n_moe = DEPTH // 2

    x_prompt = nrm(0, (BATCH, SEQ, D_MODEL), 1.0)
    x_sample = nrm(1, (DEC_BATCH, DEC_SEQ, D_MODEL), 1.0)
    cache_k = nrm(2, (DEPTH, n_pool, PAGE_SIZE, HB, DB), 1.0)
    cache_v = nrm(3, (DEPTH, n_pool, PAGE_SIZE, HB, DB), 1.0)
    page_table = jax.random.permutation(ks[4], n_pool)[:n_used].reshape(DEC_BATCH, n_pages).astype(jnp.int32)
    state_a = nrm(5, (DEPTH, DEC_BATCH, HA, DA, DA), 0.1)
    state_a_conv = nrm(6, (DEPTH, DEC_BATCH, CONV_W - 1, 3 * WA), 1.0)
    state_c = nrm(7, (DEPTH, DEC_BATCH, HC, DKC, DVC), 0.5)

    ln_in_g = 1.0 + nrm(8, (D_MODEL,), 0.02)
    ln_in_b = nrm(9, (D_MODEL,), 0.02)
    col_scale = jnp.ones((P_IN,), f32)
    col_scale = col_scale.at[2 * WA:3 * WA].set(BETA_INIT)
    col_scale = col_scale.at[OFF_B + 2 * WB:OFF_B + 3 * WB].set(BETA_INIT)
    col_scale = col_scale.at[OFF_C + 2 * KC:OFF_C + 2 * KC + WC].set(BETA_INIT)
    w_in = nrm(10, (DEPTH, D_MODEL, P_IN), D_MODEL ** -0.5) * col_scale
    w_out = nrm(11, (DEPTH, D_MODEL, D_MODEL), D_MODEL ** -0.5 * BETA_INIT)
    a_conv_w = nrm(12, (DEPTH, CONV_W, 3 * WA), CONV_W ** -0.5)
    a_a_log = jnp.log(jax.random.uniform(ks[13], (DEPTH, HA), f32, 1.0, 16.0))
    dt = jnp.exp(jax.random.uniform(ks[14], (DEPTH, HA), f32, math.log(1e-3), math.log(1e-1)))
    a_dt_bias = dt + jnp.log(-jnp.expm1(-dt))
    a_norm_w = 1.0 + nrm(15, (DEPTH, DA), 0.02)
    c_lb_logits = nrm(16, (DEPTH, KC), 0.1)
    c_norm_w = 1.0 + nrm(17, (DEPTH, DVC), 0.02)
    ln1_g = 1.0 + nrm(18, (DEPTH, D_MODEL), 0.02)
    ln1_b = nrm(19, (DEPTH, D_MODEL), 0.02)
    ln2_g = 1.0 + nrm(20, (DEPTH, D_MODEL), 0.02)
    ln2_b = nrm(21, (DEPTH, D_MODEL), 0.02)
    ffn_w1 = nrm(22, (n_dense, D_MODEL, D_FF), D_MODEL ** -0.5 * BETA_INIT)
    ffn_w3 = nrm(23, (n_dense, D_MODEL, D_FF), D_MODEL ** -0.5 * BETA_INIT)
    ffn_w2 = nrm(24, (n_dense, D_FF, D_MODEL), D_FF ** -0.5 * BETA_INIT)
    moe_router = nrm(25, (n_moe, D_MODEL, N_EXPERTS), D_MODEL ** -0.5)
    moe_w1 = nrm(26, (n_moe, N_EXPERTS, D_MODEL, D_FF), D_MODEL ** -0.5 * BETA_INIT)
    moe_w3 = nrm(27, (n_moe, N_EXPERTS, D_MODEL, D_FF), D_MODEL ** -0.5 * BETA_INIT)
    moe_w2 = nrm(28, (n_moe, N_EXPERTS, D_FF, D_MODEL), D_FF ** -0.5 * BETA_INIT)
    return {'x_prompt': x_prompt, 'x_sample': x_sample, 'cache_k': cache_k, 'cache_v': cache_v,
            'page_table': page_table, 'state_a': state_a, 'state_a_conv': state_a_conv,
            'state_c': state_c, 'ln_in_g': ln_in_g, 'ln_in_b': ln_in_b, 'w_in': w_in,
            'w_out': w_out, 'a_conv_w': a_conv_w, 'a_a_log': a_a_log, 'a_dt_bias': a_dt_bias,
            'a_norm_w': a_norm_w, 'c_lb_logits': c_lb_logits, 'c_norm_w': c_norm_w,
            'ln1_g': ln1_g, 'ln1_b': ln1_b, 'ln2_g': ln2_g, 'ln2_b': ln2_b,
            'ffn_w1': ffn_w1, 'ffn_w3': ffn_w3, 'ffn_w2': ffn_w2, 'moe_router': moe_router,
            'moe_w1': moe_w1, 'moe_w3': moe_w3, 'moe_w2': moe_w2}


def reference(x_prompt, x_sample, cache_k, cache_v, page_table, state_a, state_a_conv, state_c,
              ln_in_g, ln_in_b, w_in, w_out, a_conv_w, a_a_log, a_dt_bias, a_norm_w, c_lb_logits,
              c_norm_w, ln1_g, ln1_b, ln2_g, ln2_b, ffn_w1, ffn_w3, ffn_w2, moe_router, moe_w1,
              moe_w3, moe_w2):
    f32 = jnp.float32
    bp, tp, _ = x_prompt.shape
    bs, ts, _ = x_sample.shape
    n_pages = page_table.shape[1]
    past_len = n_pages * cache_k.shape[2]
    pos_p = jnp.arange(tp, dtype=jnp.int32)
    pos_s = past_len + jnp.arange(ts, dtype=jnp.int32)
    slopes = alibi_slopes(HB)
    lb_all = jnp.cumsum(jax.nn.softmax(c_lb_logits.astype(f32), axis=0), axis=0)
    lb_all = lb_all - lb_all[0:1]

    xp = layer_norm(x_prompt, ln_in_g, ln_in_b)
    xs = layer_norm(x_sample, ln_in_g, ln_in_b)
    sa0_p = jnp.zeros((bp, HA, DA, DA), f32)
    conv0_p = jnp.zeros((bp, CONV_W - 1, 3 * WA), f32)
    sc0_p = jnp.zeros((bp, HC, DKC, DVC), f32)
    st_p = []
    st_s = []
    for l in range(DEPTH):
        if l % 2 == 0:
            ffn = functools.partial(swiglu, w1=ffn_w1[l // 2], w3=ffn_w3[l // 2], w2=ffn_w2[l // 2])
        else:
            ffn = functools.partial(moe_ffn, router=moe_router[l // 2], w1=moe_w1[l // 2],
                                    w3=moe_w3[l // 2], w2=moe_w2[l // 2])
        lw = (w_in[l], w_out[l], a_conv_w[l], a_a_log[l], a_dt_bias[l], a_norm_w[l], lb_all[l],
              c_norm_w[l], ln1_g[l], ln1_b[l], ln2_g[l], ln2_b[l])
        xp, sp = decoder_layer(xp, pos_p, None, None, sa0_p, conv0_p, sc0_p, *lw, ffn=ffn, slopes=slopes)
        st_p.append(sp)
        k_past = cache_k[l][page_table].reshape(bs, past_len, HB, DB)
        v_past = cache_v[l][page_table].reshape(bs, past_len, HB, DB)
        xs, ss = decoder_layer(xs, pos_s, k_past, v_past, state_a[l], state_a_conv[l], state_c[l],
                               *lw, ffn=ffn, slopes=slopes)
        st_s.append(ss)

    k_prompt = jnp.stack([s[0] for s in st_p])
    v_prompt = jnp.stack([s[1] for s in st_p])
    sa_prompt = jnp.stack([s[2] for s in st_p])
    conv_prompt = jnp.stack([s[3] for s in st_p])
    sc_prompt = jnp.stack([s[4] for s in st_p])
    k_sample = jnp.stack([s[0] for s in st_s])
    v_sample = jnp.stack([s[1] for s in st_s])
    sa_sample = jnp.stack([s[2] for s in st_s])
    conv_sample = jnp.stack([s[3] for s in st_s])
    sc_sample = jnp.stack([s[4] for s in st_s])
    return (xp, xs, k_prompt, v_prompt, k_sample, v_sample, sa_prompt, sa_sample,
            conv_prompt, conv_sample, sc_prompt, sc_sample)
```

```python
import functools
import math

import numpy as np
import jax
import jax.numpy as jnp
from jax import lax
from jax.experimental import pallas as pl
from jax.experimental.pallas import tpu as pltpu

HEAD_DIM = 64
CONV_W = 4
CHUNK = 64
MOBA_BLOCK = 256
MOBA_TOPK = 3
MOE_TOPK = 2
LN_EPS = 1e-5
RMS_EPS = 1e-6
LANES = 128
VMEM_LIMIT = 56 * 1024 * 1024
HI = lax.Precision.HIGHEST
F32 = jnp.float32
BF16 = jnp.bfloat16


def _cparams(sem):
    return pltpu.CompilerParams(dimension_semantics=sem, vmem_limit_bytes=VMEM_LIMIT)


def _bdot(a, b):
    return jnp.dot(a.astype(BF16), b.astype(BF16), preferred_element_type=F32)


def _es(spec, a, b):
    return jnp.einsum(spec, a, b, precision=HI, preferred_element_type=F32)


def _layer_norm(x, g, b):
    mu = jnp.mean(x, -1, keepdims=True)
    xc = x - mu
    var = jnp.mean(xc * xc, -1, keepdims=True)
    return xc * lax.rsqrt(var + LN_EPS) * g + b


def _sigmoid(x):
    return 1.0 / (1.0 + jnp.exp(-x))


def _silu(x):
    return x * _sigmoid(x)


def _softplus(x):
    return jnp.maximum(x, 0.0) + jnp.log1p(jnp.exp(-jnp.abs(x)))


def _ln_kernel(x_ref, g_ref, b_ref, o_ref):
    o_ref[...] = _layer_norm(x_ref[...], g_ref[...], b_ref[...])


def _ln_rows(x, g, b, tm):
    n, d = x.shape
    return pl.pallas_call(
        _ln_kernel,
        grid=(n // tm,),
        in_specs=[pl.BlockSpec((tm, d), lambda i: (i, 0)),
                  pl.BlockSpec((1, d), lambda i: (0, 0)),
                  pl.BlockSpec((1, d), lambda i: (0, 0))],
        out_specs=pl.BlockSpec((tm, d), lambda i: (i, 0)),
        out_shape=jax.ShapeDtypeStruct((n, d), F32),
        compiler_params=_cparams(("parallel",)),
        name="ln_rows",
    )(x, g.reshape(1, d), b.reshape(1, d))


def _in_proj_kernel(x_ref, w_ref, *o_refs, segs):
    xb = x_ref[...].astype(BF16)
    for o_ref, (lo, hi) in zip(o_refs, segs):
        o_ref[...] = jnp.dot(xb, w_ref[:, lo:hi], preferred_element_type=F32)


def _in_proj(x, w, segs, tm):
    n, d = x.shape
    return pl.pallas_call(
        functools.partial(_in_proj_kernel, segs=segs),
        grid=(n // tm,),
        in_specs=[pl.BlockSpec((tm, d), lambda i: (i, 0)),
                  pl.BlockSpec(w.shape, lambda i: (0, 0))],
        out_specs=[pl.BlockSpec((tm, hi - lo), lambda i: (i, 0)) for lo, hi in segs],
        out_shape=[jax.ShapeDtypeStruct((n, hi - lo), F32) for lo, hi in segs],
        compiler_params=_cparams(("parallel",)),
        name="in_proj",
    )(x, w)


def _heads(a, nh):
    nb, c, _ = a.shape
    d = HEAD_DIM
    st = jnp.stack([a[:, :, h * d:(h + 1) * d] for h in range(nh)], axis=1)
    return st.reshape(nb * nh, c, d)


def _unheads(o, nb, nh):
    _, c, d = o.shape
    o4 = o.reshape(nb, nh, c, d)
    return jnp.concatenate([o4[:, h] for h in range(nh)], axis=-1)


def _gated_rms(o, w, z):
    o = o * lax.rsqrt(jnp.mean(o * o, -1, keepdims=True) + RMS_EPS)
    return o * w * _silu(z)


def _mixa_prompt_kernel(h_ref, ab_ref, cw_ref, al_ref, dtb_ref, nw_ref,
                        o_ref, sa_ref, s_scr, prev_scr, *, nb, nh):
    c = pl.program_id(0)
    C, D = CHUNK, HEAD_DIM
    W = nh * D
    G = nb * nh

    @pl.when(c == 0)
    def _():
        s_scr[...] = jnp.zeros_like(s_scr)
        prev_scr[...] = jnp.zeros_like(prev_scr)

    x = h_ref[:, :, 0:3 * W]
    z = h_ref[:, :, 3 * W:4 * W]
    xp = jnp.concatenate([prev_scr[...], x], axis=1)
    cw = cw_ref[...]
    y = (xp[:, 5:5 + C] * cw[0] + xp[:, 6:6 + C] * cw[1]
         + xp[:, 7:7 + C] * cw[2] + xp[:, 8:8 + C] * cw[3])
    prev_scr[...] = x[:, C - 8:C]
    y = _silu(y)

    q = _heads(y[:, :, 0:W], nh)
    k = _heads(y[:, :, W:2 * W], nh)
    v = _heads(y[:, :, 2 * W:3 * W], nh)
    zh = _heads(z, nh)
    q = q * lax.rsqrt(jnp.sum(q * q, -1, keepdims=True) + 1e-6) * (D ** -0.5)
    k = k * lax.rsqrt(jnp.sum(k * k, -1, keepdims=True) + 1e-6)

    ab = ab_ref[...]
    dt = _softplus(ab + dtb_ref[...])
    gfull = -jnp.exp(al_ref[...]) * dt
    bfull = _sigmoid(ab)

    def lane_bc(a, off):
        st = jnp.stack([jnp.broadcast_to(a[:, :, off + h:off + h + 1], (nb, C, D))
                        for h in range(nh)], axis=1)
        return st.reshape(G, C, D)

    gb = lane_bc(gfull, 0)
    beta = lane_bc(bfull, nh)

    ii = lax.broadcasted_iota(jnp.int32, (C, C), 0)
    jj = lax.broadcasted_iota(jnp.int32, (C, C), 1)

    def bcg(m):
        return jnp.broadcast_to(m.astype(F32), (G, C, C))

    gam = _es('gij,gjk->gik', bcg(ii >= jj), gb)
    gam_t = _es('gik,gjk->gij', bcg(jj == 0), gam)
    dec = jnp.exp(jnp.where(ii >= jj, gam - gam_t, -jnp.inf))

    kb = k * beta
    nmat = jnp.where(ii > jj, _es('gik,gjk->gij', kb, k) * dec, 0.0)
    xinv = (ii == jj).astype(F32) - jnp.where(ii // 2 == jj // 2, nmat, 0.0)
    s = 4
    while s <= C:
        off = jnp.where((ii // s == jj // s) & (ii // (s // 2) != jj // (s // 2)), nmat, 0.0)
        xinv = xinv - _es('gij,gjk->gik', xinv, _es('gij,gjk->gik', off, xinv))
        s *= 2

    egam = jnp.exp(gam)
    rhs = jnp.concatenate([v * beta, kb * egam], axis=-1)
    sol = _es('gij,gjk->gik', xinv, rhs)
    st = s_scr[...]
    u = sol[:, :, 0:D] - _es('gck,gkv->gcv', sol[:, :, D:2 * D], st)
    attn = _es('gik,gjk->gij', q, k) * dec
    o = _es('gck,gkv->gcv', q * egam, st) + _es('gij,gjv->giv', attn, u)
    glast = gam[:, C - 1:C, :]
    kdec = k * jnp.exp(glast - gam)
    st_new = st * jnp.exp(glast) + _es('gck,gcv->gkv', kdec, u)
    s_scr[...] = st_new

    o = _gated_rms(o, nw_ref[...], zh)
    o_ref[...] = _unheads(o, nb, nh)

    @pl.when(c == pl.num_programs(0) - 1)
    def _():
        sa_ref[...] = st_new.reshape(nb, nh, D, D)


def _mixa_prompt(h_a, h_ab, cw, a_log, dt_bias, norm_w, nh):
    nb, t, wa4 = h_a.shape
    C, D = CHUNK, HEAD_DIM
    W = nh * D
    pad = LANES - nh
    al = jnp.pad(a_log, (0, pad)).reshape(1, LANES)
    dtb = jnp.pad(dt_bias, (0, pad)).reshape(1, LANES)
    return pl.pallas_call(
        functools.partial(_mixa_prompt_kernel, nb=nb, nh=nh),
        grid=(t // C,),
        in_specs=[pl.BlockSpec((nb, C, wa4), lambda c: (0, c, 0)),
                  pl.BlockSpec((nb, C, LANES), lambda c: (0, c, 0)),
                  pl.BlockSpec((CONV_W, 3 * W), lambda c: (0, 0)),
                  pl.BlockSpec((1, LANES), lambda c: (0, 0)),
                  pl.BlockSpec((1, LANES), lambda c: (0, 0)),
                  pl.BlockSpec((1, D), lambda c: (0, 0))],
        out_specs=[pl.BlockSpec((nb, C, W), lambda c: (0, c, 0)),
                   pl.BlockSpec((nb, nh, D, D), lambda c: (0, 0, 0, 0))],
        out_shape=[jax.ShapeDtypeStruct((nb, t, W), F32),
                   jax.ShapeDtypeStruct((nb, nh, D, D), F32)],
        scratch_shapes=[pltpu.VMEM((nb * nh, D, D), F32),
                        pltpu.VMEM((nb, 8, 3 * W), F32)],
        compiler_params=_cparams(("arbitrary",)),
        name="mixa_prompt",
    )(h_a, h_ab, cw, al, dtb, norm_w.reshape(1, D))


def _hgrn_prompt_kernel(h_ref, loglb_ref, log1mlb_ref, omlb_ref, nw_ref,
                        o_ref, sc_ref, st_scr, *, nb, nh):
    c = pl.program_id(0)
    C, D = CHUNK, HEAD_DIM
    W = nh * D
    G = nb * nh

    @pl.when(c == 0)
    def _():
        st_scr[...] = jnp.zeros_like(st_scr)

    x = h_ref[...]
    qx = _heads(x[:, :, 0:W], nh)
    fx = _heads(x[:, :, W:2 * W], nh)
    iv = _heads(x[:, :, 2 * W:3 * W], nh)
    zh = _heads(x[:, :, 3 * W:4 * W], nh)

    la = loglb_ref[...]
    lbb = log1mlb_ref[...] - _softplus(-fx)
    mx = jnp.maximum(la, lbb)
    logf = mx + jnp.log1p(jnp.exp(-jnp.abs(la - lbb)))
    kc = omlb_ref[...] * _sigmoid(-fx)
    q = _silu(qx)

    ii = lax.broadcasted_iota(jnp.int32, (C, C), 0)
    jj = lax.broadcasted_iota(jnp.int32, (C, C), 1)
    ri = lax.broadcasted_iota(jnp.int32, (C, D), 0)
    gam = _es('gij,gjk->gik', jnp.broadcast_to((ii >= jj).astype(F32), (G, C, C)), logf)

    def block_starts(hb):
        g4 = gam.reshape(G, C // hb, hb, D)
        starts = g4[:, :, 0:1, :]
        nxt = jnp.concatenate([starts[:, 1:], starts[:, -1:]], axis=1)
        own = jnp.broadcast_to(starts, g4.shape).reshape(G, C, D)
        nx = jnp.broadcast_to(nxt, g4.shape).reshape(G, C, D)
        return own, nx

    attn = jnp.zeros((G, C, C), F32)
    hb = C // 2
    while hb >= 8:
        s = 2 * hb
        own, nx = block_starts(hb)
        upper = (ri % s) >= hb
        qs = jnp.where(upper, q * jnp.exp(gam - own), 0.0)
        ks = jnp.where(upper, 0.0, kc * jnp.exp(jnp.where(upper, 0.0, nx - gam)))
        attn = attn + jnp.where(ii // s == jj // s, _es('gik,gjk->gij', qs, ks), 0.0)
        hb //= 2
    own, _ = block_starts(8)
    qd = q * jnp.exp(gam - own)
    kd = kc * jnp.exp(own - gam)
    attn = attn + jnp.where((ii // 8 == jj // 8) & (ii >= jj), _es('gik,gjk->gij', qd, kd), 0.0)

    st = st_scr[...]
    o = _es('gck,gvk->gcv', q * jnp.exp(gam), st) + _es('gij,gjv->giv', attn, iv)
    glast = gam[:, C - 1:C, :]
    kdec = kc * jnp.exp(glast - gam)
    st_new = st * jnp.exp(glast) + _es('gcv,gck->gvk', iv, kdec)
    st_scr[...] = st_new

    o = _gated_rms(o, nw_ref[...], zh)
    o_ref[...] = _unheads(o, nb, nh)

    @pl.when(c == pl.num_programs(0) - 1)
    def _():
        di = lax.broadcasted_iota(jnp.int32, (D, D), 0)
        dj = lax.broadcasted_iota(jnp.int32, (D, D), 1)
        eye = jnp.broadcast_to((di == dj).astype(F32), (G, D, D))
        sc_ref[...] = _es('gkj,gvj->gkv', eye, st_new).reshape(nb, nh, D, D)


def _hgrn_prompt(h_c, lb, norm_w, nh):
    nb, t, wc4 = h_c.shape
    C, D = CHUNK, HEAD_DIM
    W = nh * D
    G = nb * nh

    def per_group(a):
        return jnp.tile(a.reshape(nh, 1, D), (nb, 1, 1))

    return pl.pallas_call(
        functools.partial(_hgrn_prompt_kernel, nb=nb, nh=nh),
        grid=(t // C,),
        in_specs=[pl.BlockSpec((nb, C, wc4), lambda c: (0, c, 0)),
                  pl.BlockSpec((G, 1, D), lambda c: (0, 0, 0)),
                  pl.BlockSpec((G, 1, D), lambda c: (0, 0, 0)),
                  pl.BlockSpec((G, 1, D), lambda c: (0, 0, 0)),
                  pl.BlockSpec((1, D), lambda c: (0, 0))],
        out_specs=[pl.BlockSpec((nb, C, W), lambda c: (0, c, 0)),
                   pl.BlockSpec((nb, nh, D, D), lambda c: (0, 0, 0, 0))],
        out_shape=[jax.ShapeDtypeStruct((nb, t, W), F32),
                   jax.ShapeDtypeStruct((nb, nh, D, D), F32)],
        scratch_shapes=[pltpu.VMEM((G, D, D), F32)],
        compiler_params=_cparams(("arbitrary",)),
        name="hgrn_prompt",
    )(h_c, per_group(jnp.log(lb)), per_group(jnp.log1p(-lb)), per_group(1.0 - lb),
      norm_w.reshape(1, D))


def _mixa_sample_kernel(al_ref, dtb_ref, xq_ref, xk_ref, xv_ref, z_ref, ab_ref,
                        cwq_ref, cwk_ref, cwv_ref, nw_ref, s0_ref,
                        o_ref, s_ref, kq_scr, *, nh, ts):
    h = pl.program_id(0)
    D = HEAD_DIM
    s_ref[...] = s0_ref[...]
    neg_a = -jnp.exp(al_ref[h])
    dtb = dtb_ref[h]

    def conv(x_ref, cw_ref, t):
        acc = x_ref[t] * cw_ref[0]
        for j in range(1, CONV_W):
            acc = acc + x_ref[t + j] * cw_ref[j]
        return _silu(acc)

    for t in range(ts):
        q = conv(xq_ref, cwq_ref, t)
        k = conv(xk_ref, cwk_ref, t)
        v = conv(xv_ref, cwv_ref, t)
        q = q * lax.rsqrt(jnp.sum(q * q, 0, keepdims=True) + 1e-6) * (D ** -0.5)
        k = k * lax.rsqrt(jnp.sum(k * k, 0, keepdims=True) + 1e-6)
        a = ab_ref[t, pl.ds(h, 1), :]
        b = ab_ref[t, pl.ds(nh + h, 1), :]
        decay = jnp.exp(neg_a * _softplus(a + dtb))
        beta = _sigmoid(b)
        kq_scr[0] = k
        kq_scr[1] = q

        def pass1(d, r):
            kd = kq_scr[0, pl.ds(d, 1), :]
            return r + (s_ref[0, d] * decay) * kd

        r = lax.fori_loop(0, D, pass1, jnp.zeros_like(v))
        vn = beta * (v - r)

        def pass2(d, o):
            kd = kq_scr[0, pl.ds(d, 1), :]
            qd = kq_scr[1, pl.ds(d, 1), :]
            sd = s_ref[0, d] * decay + kd * vn
            s_ref[0, d] = sd
            return o + sd * qd

        o = lax.fori_loop(0, D, pass2, jnp.zeros_like(v))
        o = o * lax.rsqrt(jnp.mean(o * o, 0, keepdims=True) + RMS_EPS)
        o_ref[t] = o * nw_ref[...] * _silu(z_ref[t])


def _mixa_sample(x_t, z_t, ab_t, cw_t, a_log, dt_bias, norm_w, s0_t, nh, ts):
    D = HEAD_DIM
    W = nh * D
    nb = x_t.shape[-1]
    tt = x_t.shape[0]
    nab = ab_t.shape[1]
    smem = pl.BlockSpec(memory_space=pltpu.SMEM)
    return pl.pallas_call(
        functools.partial(_mixa_sample_kernel, nh=nh, ts=ts),
        grid=(nh,),
        in_specs=[smem, smem,
                  pl.BlockSpec((tt, D, nb), lambda h: (0, h, 0)),
                  pl.BlockSpec((tt, D, nb), lambda h: (0, nh + h, 0)),
                  pl.BlockSpec((tt, D, nb), lambda h: (0, 2 * nh + h, 0)),
                  pl.BlockSpec((ts, D, nb), lambda h: (0, h, 0)),
                  pl.BlockSpec((ts, nab, nb), lambda h: (0, 0, 0)),
                  pl.BlockSpec((CONV_W, D, 1), lambda h: (0, h, 0)),
                  pl.BlockSpec((CONV_W, D, 1), lambda h: (0, nh + h, 0)),
                  pl.BlockSpec((CONV_W, D, 1), lambda h: (0, 2 * nh + h, 0)),
                  pl.BlockSpec((D, 1), lambda h: (0, 0)),
                  pl.BlockSpec((1, D, D, nb), lambda h: (h, 0, 0, 0))],
        out_specs=[pl.BlockSpec((ts, D, nb), lambda h: (0, h, 0)),
                   pl.BlockSpec((1, D, D, nb), lambda h: (h, 0, 0, 0))],
        out_shape=[jax.ShapeDtypeStruct((ts, W, nb), F32),
                   jax.ShapeDtypeStruct((nh, D, D, nb), F32)],
        scratch_shapes=[pltpu.VMEM((2, D, nb), F32)],
        compiler_params=_cparams(("arbitrary",)),
        name="mixa_sample",
    )(a_log, dt_bias, x_t, x_t, x_t, z_t, ab_t, cw_t, cw_t, cw_t,
      norm_w.reshape(D, 1), s0_t)


def _hgrn_sample_kernel(q_ref, f_ref, i_ref, z_ref, lb_ref, nw_ref, s0_ref,
                        o_ref, s_ref, kq_scr, *, ts):
    D = HEAD_DIM
    s_ref[...] = s0_ref[...]
    lb = lb_ref[...]
    for t in range(ts):
        fx = f_ref[t]
        f = lb + (1.0 - lb) * _sigmoid(fx)
        kq_scr[0] = f
        kq_scr[1] = (1.0 - lb) * _sigmoid(-fx)
        kq_scr[2] = _silu(q_ref[t])
        v = i_ref[t]

        def body(d, o):
            fd = kq_scr[0, pl.ds(d, 1), :]
            kd = kq_scr[1, pl.ds(d, 1), :]
            qd = kq_scr[2, pl.ds(d, 1), :]
            sd = s_ref[0, d] * fd + kd * v
            s_ref[0, d] = sd
            return o + sd * qd

        o = lax.fori_loop(0, D, body, jnp.zeros_like(v))
        o = o * lax.rsqrt(jnp.mean(o * o, 0, keepdims=True) + RMS_EPS)
        o_ref[t] = o * nw_ref[...] * _silu(z_ref[t])


def _hgrn_sample(hc_t, lb, norm_w, s0_t, nh, ts):
    D = HEAD_DIM
    W = nh * D
    nb = hc_t.shape[-1]
    return pl.pallas_call(
        functools.partial(_hgrn_sample_kernel, ts=ts),
        grid=(nh,),
        in_specs=[pl.BlockSpec((ts, D, nb), lambda h: (0, h, 0)),
                  pl.BlockSpec((ts, D, nb), lambda h: (0, nh + h, 0)),
                  pl.BlockSpec((ts, D, nb), lambda h: (0, 2 * nh + h, 0)),
                  pl.BlockSpec((ts, D, nb), lambda h: (0, 3 * nh + h, 0)),
                  pl.BlockSpec((D, 1), lambda h: (h, 0)),
                  pl.BlockSpec((D, 1), lambda h: (0, 0)),
                  pl.BlockSpec((1, D, D, nb), lambda h: (h, 0, 0, 0))],
        out_specs=[pl.BlockSpec((ts, D, nb), lambda h: (0, h, 0)),
                   pl.BlockSpec((1, D, D, nb), lambda h: (h, 0, 0, 0))],
        out_shape=[jax.ShapeDtypeStruct((ts, W, nb), F32),
                   jax.ShapeDtypeStruct((nh, D, D, nb), F32)],
        scratch_shapes=[pltpu.VMEM((3, D, nb), F32)],
        compiler_params=_cparams(("arbitrary",)),
        name="hgrn_sample",
    )(hc_t, hc_t, hc_t, hc_t, lb.reshape(W, 1), norm_w.reshape(D, 1), s0_t)


def _block_mean_kernel(k_ref, o_ref):
    o_ref[...] = jnp.mean(k_ref[...], axis=0, keepdims=True)


def _block_mean(k):
    nb, t, w = k.shape
    nblk = t // MOBA_BLOCK
    out = pl.pallas_call(
        _block_mean_kernel,
        grid=(nb, nblk),
        in_specs=[pl.BlockSpec((None, MOBA_BLOCK, w), lambda b, n: (b, n, 0))],
        out_specs=pl.BlockSpec((None, None, 1, w), lambda b, n: (b, n, 0, 0)),
        out_shape=jax.ShapeDtypeStruct((nb, nblk, 1, w), F32),
        compiler_params=_cparams(("parallel", "parallel")),
        name="moba_block_mean",
    )(k)
    return out.reshape(nb, nblk, w)


def _alibi_slope(h, nh):
    return float(2.0 ** (-8.0 * (h + 1) / nh))


def _moba_prompt_kernel(qi_ref, kj_ref, q_ref, k_ref, v_ref, km_ref, o_ref,
                        acc_scr, m_scr, l_scr, sel_scr, *, nh, nblk):
    step = pl.program_id(1)
    qi = qi_ref[step]
    kj = kj_ref[step]
    BS, D = MOBA_BLOCK, HEAD_DIM
    scale = D ** -0.5

    @pl.when(kj == 0)
    def _():
        m_scr[...] = jnp.full_like(m_scr, -jnp.inf)
        l_scr[...] = jnp.zeros_like(l_scr)
        acc_scr[...] = jnp.zeros_like(acc_scr)
        nn = lax.broadcasted_iota(jnp.int32, (nblk, BS), 0)
        elig = nn < qi
        for h in range(nh):
            gate = _es('nd,qd->nq', km_ref[:, h * D:(h + 1) * D], q_ref[:, h * D:(h + 1) * D])
            gate = jnp.where(elig, gate, -jnp.inf)
            rank = jnp.zeros((nblk, BS), F32)
            for m in range(nblk):
                gm = gate[m:m + 1, :]
                ahead = (gm > gate) | ((gm == gate) & (nn > m))
                rank = rank + jnp.where(ahead, 1.0, 0.0)
            sel_scr[h] = jnp.where(elig & (rank < MOBA_TOPK), 1.0, 0.0)

    kpos = lax.broadcasted_iota(jnp.int32, (BS, BS), 0)
    qpos = lax.broadcasted_iota(jnp.int32, (BS, BS), 1)
    rel = (qpos - kpos + (qi - kj) * BS).astype(F32)
    own = qi == kj
    for h in range(nh):
        qh = q_ref[:, h * D:(h + 1) * D]
        kh = k_ref[:, h * D:(h + 1) * D]
        vh = v_ref[:, h * D:(h + 1) * D]
        st = jnp.einsum('kd,qd->kq', kh.astype(BF16), qh.astype(BF16),
                        preferred_element_type=F32) * scale
        picked = sel_scr[h, pl.ds(kj, 1), :] > 0.0
        ok = (rel >= 0.0) & (picked | own)
        st = jnp.where(ok, st - _alibi_slope(h, nh) * rel, -jnp.inf)
        m_old = m_scr[h:h + 1, :]
        m_new = jnp.maximum(m_old, jnp.max(st, axis=0, keepdims=True))
        m_safe = jnp.where(m_new == -jnp.inf, 0.0, m_new)
        p = jnp.exp(st - m_safe)
        alpha = jnp.exp(m_old - m_safe)
        l_scr[h:h + 1, :] = alpha * l_scr[h:h + 1, :] + jnp.sum(p, axis=0, keepdims=True)
        m_scr[h:h + 1, :] = m_new
        pv = jnp.einsum('kd,kq->dq', vh.astype(BF16), p.astype(BF16),
                        preferred_element_type=F32)
        acc_scr[h] = alpha * acc_scr[h] + pv

    @pl.when(own)
    def _():
        outs = [acc_scr[h] / l_scr[h:h + 1, :] for h in range(nh)]
        o_ref[...] = jnp.concatenate(outs, axis=0).T


def _moba_prompt(q, k, v, nh):
    nb, t, w = q.shape
    BS, D = MOBA_BLOCK, HEAD_DIM
    nblk = t // BS
    km = _block_mean(k)
    pairs = [(i, j) for i in range(nblk) for j in range(i + 1)]
    qi = jnp.asarray([p[0] for p in pairs], jnp.int32)
    kj = jnp.asarray([p[1] for p in pairs], jnp.int32)
    grid_spec = pltpu.PrefetchScalarGridSpec(
        num_scalar_prefetch=2,
        grid=(nb, len(pairs)),
        in_specs=[pl.BlockSpec((None, BS, w), lambda b, s, qi, kj: (b, qi[s], 0)),
                  pl.BlockSpec((None, BS, w), lambda b, s, qi, kj: (b, kj[s], 0)),
                  pl.BlockSpec((None, BS, w), lambda b, s, qi, kj: (b, kj[s], 0)),
                  pl.BlockSpec((None, nblk, w), lambda b, s, qi, kj: (b, 0, 0))],
        out_specs=pl.BlockSpec((None, BS, w), lambda b, s, qi, kj: (b, qi[s], 0)),
        scratch_shapes=[pltpu.VMEM((nh, D, BS), F32),
                        pltpu.VMEM((nh, BS), F32),
                        pltpu.VMEM((nh, BS), F32),
                        pltpu.VMEM((nh, nblk, BS), F32)])
    return pl.pallas_call(
        functools.partial(_moba_prompt_kernel, nh=nh, nblk=nblk),
        grid_spec=grid_spec,
        out_shape=jax.ShapeDtypeStruct((nb, t, w), F32),
        compiler_params=_cparams(("parallel", "arbitrary")),
        name="moba_prompt",
    )(qi, kj, q, k, v, km)


def _moba_sample_kernel(pt_ref, q_ref, kn_ref, vn_ref, ka_ref, kb_ref, va_ref, vb_ref,
                        o_ref, m_scr, l_scr, g_scr, acc_scr, *, nh, ts, past_len):
    n = pl.program_id(1)
    nblk = pl.num_programs(1)
    BS, D = MOBA_BLOCK, HEAD_DIM
    W = nh * D
    R = ts * nh
    scale = D ** -0.5
    q = q_ref[...]
    row = lax.broadcasted_iota(jnp.int32, (R, W), 0)
    col = lax.broadcasted_iota(jnp.int32, (R, W), 1)
    head_mask = (row % nh) == (col // D)
    qrep = jnp.concatenate([jnp.broadcast_to(q[t:t + 1, :], (nh, W)) for t in range(ts)], axis=0)
    qbd = jnp.where(head_mask, qrep, 0.0)

    r1 = lax.broadcasted_iota(jnp.int32, (R, 1), 0)
    tpos = (r1 // nh).astype(F32)
    hrow = r1 % nh
    slope = jnp.zeros((R, 1), F32)
    for h in range(nh):
        slope = jnp.where(hrow == h, _alibi_slope(h, nh), slope)

    kblk = jnp.concatenate([ka_ref[...], kb_ref[...]], axis=0)
    vblk = jnp.concatenate([va_ref[...], vb_ref[...]], axis=0)
    s = jnp.einsum('rc,kc->rk', qbd.astype(BF16), kblk.astype(BF16),
                   preferred_element_type=F32) * scale
    lane = lax.broadcasted_iota(jnp.int32, (R, BS), 1)
    dist = (past_len + tpos) - (n * BS + lane).astype(F32)
    s = s - slope * dist
    m_n = jnp.max(s, axis=-1, keepdims=True)
    p = jnp.exp(s - m_n)
    m_scr[n] = jnp.broadcast_to(m_n, (R, LANES))
    l_scr[n] = jnp.broadcast_to(jnp.sum(p, axis=-1, keepdims=True), (R, LANES))
    acc_scr[n] = jnp.dot(p.astype(BF16), vblk.astype(BF16), preferred_element_type=F32)
    kmean = jnp.mean(kblk, axis=0, keepdims=True)
    g_scr[n] = jnp.broadcast_to(jnp.sum(qbd * kmean, axis=-1, keepdims=True), (R, LANES))

    @pl.when(n == nblk - 1)
    def _():
        nb_past = past_len // BS
        gates = [g_scr[i][:, 0:1] for i in range(nb_past)]
        sels = []
        for i in range(nb_past):
            rank = jnp.zeros((R, 1), F32)
            for m in range(nb_past):
                if m == i:
                    continue
                ahead = (gates[m] > gates[i]) if m > i else (gates[m] >= gates[i])
                rank = rank + jnp.where(ahead, 1.0, 0.0)
            sels.append(rank < MOBA_TOPK)
        kn = kn_ref[...]
        vn = vn_ref[...]
        s_own = []
        for j in range(ts):
            sj = jnp.sum(qbd * kn[j:j + 1, :], axis=-1, keepdims=True) * scale
            sj = sj - slope * (tpos - j)
            s_own.append(jnp.where(tpos >= j, sj, -jnp.inf))
        m_tot = s_own[0]
        for j in range(1, ts):
            m_tot = jnp.maximum(m_tot, s_own[j])
        for i in range(nb_past):
            m_tot = jnp.maximum(m_tot, jnp.where(sels[i], m_scr[i][:, 0:1], -jnp.inf))
        num = jnp.zeros((R, W), F32)
        den = jnp.zeros((R, 1), F32)
        for j in range(ts):
            pj = jnp.exp(s_own[j] - m_tot)
            num = num + pj * vn[j:j + 1, :]
            den = den + pj
        for i in range(nb_past):
            wi = jnp.where(sels[i], jnp.exp(m_scr[i][:, 0:1] - m_tot), 0.0)
            num = num + wi * acc_scr[i]
            den = den + wi * l_scr[i][:, 0:1]
        out = jnp.where(head_mask, num / den, 0.0)
        o_ref[...] = jnp.sum(out.reshape(ts, nh, W), axis=1)


def _moba_sample(q, kn, vn, cache_k, cache_v, page_table, layer, nh):
    nb, ts, w = q.shape
    page = cache_k.shape[2]
    n_pages = page_table.shape[1]
    past_len = n_pages * page
    BS, D = MOBA_BLOCK, HEAD_DIM
    assert BS == 2 * page and past_len % BS == 0 and ts <= BS
    nblk = past_len // BS
    R = ts * nh
    pt = page_table.reshape(-1)

    def page_spec(off):
        return pl.BlockSpec((None, None, page, w),
                            lambda b, n, pt: (layer, pt[b * n_pages + 2 * n + off], 0, 0))

    tok = pl.BlockSpec((None, ts, w), lambda b, n, pt: (b, 0, 0))
    grid_spec = pltpu.PrefetchScalarGridSpec(
        num_scalar_prefetch=1,
        grid=(nb, nblk),
        in_specs=[tok, tok, tok, page_spec(0), page_spec(1), page_spec(0), page_spec(1)],
        out_specs=tok,
        scratch_shapes=[pltpu.VMEM((nblk, R, LANES), F32),
                        pltpu.VMEM((nblk, R, LANES), F32),
                        pltpu.VMEM((nblk, R, LANES), F32),
                        pltpu.VMEM((nblk, R, w), F32)])
    return pl.pallas_call(
        functools.partial(_moba_sample_kernel, nh=nh, ts=ts, past_len=past_len),
        grid_spec=grid_spec,
        out_shape=jax.ShapeDtypeStruct((nb, ts, w), F32),
        compiler_params=_cparams(("parallel", "arbitrary")),
        name="moba_sample",
    )(pt, q, kn, vn, cache_k, cache_k, cache_v, cache_v)


def _router_gates(y, r, n_exp):
    logits = jnp.dot(y, r, precision=HI, preferred_element_type=F32)
    lane = lax.broadcasted_iota(jnp.int32, logits.shape, 1)
    logits = jnp.where(lane < n_exp, logits, -jnp.inf)
    m1 = jnp.max(logits, -1, keepdims=True)
    i1 = jnp.min(jnp.where(logits == m1, lane, LANES), -1, keepdims=True)
    first = lane == i1
    rest = jnp.where(first, -jnp.inf, logits)
    m2 = jnp.max(rest, -1, keepdims=True)
    i2 = jnp.min(jnp.where(rest == m2, lane, LANES), -1, keepdims=True)
    second = lane == i2
    e = jnp.exp(m2 - m1)
    den = 1.0 + e
    return jnp.where(first, 1.0 / den, 0.0) + jnp.where(second, e / den, 0.0)


def _out_proj_kernel(*refs, alpha, wa, wb, n_exp):
    if n_exp:
        x_ref, oa_ref, ob_ref, oc_ref, w_ref, g_ref, b_ref, r_ref, o_ref, gate_ref = refs
    else:
        x_ref, oa_ref, ob_ref, oc_ref, w_ref, g_ref, b_ref, o_ref = refs
    acc = jnp.dot(oa_ref[...].astype(BF16), w_ref[0:wa, :], preferred_element_type=F32)
    acc = acc + jnp.dot(ob_ref[...].astype(BF16), w_ref[wa:wa + wb, :], preferred_element_type=F32)
    acc = acc + jnp.dot(oc_ref[...].astype(BF16), w_ref[wa + wb:, :], preferred_element_type=F32)
    y = _layer_norm(alpha * x_ref[...] + acc, g_ref[...], b_ref[...])
    o_ref[...] = y
    if n_exp:
        gate_ref[...] = _router_gates(y, r_ref[...], n_exp)


def _out_proj(x, oa, ob, oc, w, g, b, router, alpha, tm):
    n, d = x.shape
    wa, wb, wc = oa.shape[1], ob.shape[1], oc.shape[1]
    n_exp = 0 if router is None else router.shape[1]
    row = lambda width: pl.BlockSpec((tm, width), lambda i: (i, 0))
    full = lambda shape: pl.BlockSpec(shape, lambda i: (0, 0))
    in_specs = [row(d), row(wa), row(wb), row(wc), full(w.shape), full((1, d)), full((1, d))]
    args = [x, oa, ob, oc, w, g.reshape(1, d), b.reshape(1, d)]
    out_specs = [row(d)]
    out_shape = [jax.ShapeDtypeStruct((n, d), F32)]
    if n_exp:
        in_specs.append(full((d, LANES)))
        args.append(jnp.pad(router, ((0, 0), (0, LANES - n_exp))))
        out_specs.append(row(LANES))
        out_shape.append(jax.ShapeDtypeStruct((n, LANES), F32))
    res = pl.pallas_call(
        functools.partial(_out_proj_kernel, alpha=alpha, wa=wa, wb=wb, n_exp=n_exp),
        grid=(n // tm,),
        in_specs=in_specs, out_specs=out_specs, out_shape=out_shape,
        compiler_params=_cparams(("parallel",)),
        name="out_proj_ln",
    )(*args)
    return (res[0], res[1]) if n_exp else (res[0], None)


def _ffn_kernel(x_ref, w1_ref, w3_ref, w2_ref, g_ref, b_ref, o_ref, acc_scr, xb_scr, *, alpha):
    f = pl.program_id(1)

    @pl.when(f == 0)
    def _():
        acc_scr[...] = jnp.zeros_like(acc_scr)
        xb_scr[...] = x_ref[...].astype(BF16)

    xb = xb_scr[...]
    h1 = jnp.dot(xb, w1_ref[...].astype(BF16), preferred_element_type=F32)
    h3 = jnp.dot(xb, w3_ref[...].astype(BF16), preferred_element_type=F32)
    hid = (_silu(h1) * h3).astype(BF16)
    acc_scr[...] += jnp.dot(hid, w2_ref[...].astype(BF16), preferred_element_type=F32)

    @pl.when(f == pl.num_programs(1) - 1)
    def _():
        o_ref[...] = _layer_norm(alpha * x_ref[...] + acc_scr[...], g_ref[...], b_ref[...])


def _ffn_dense(x, w1, w3, w2, g, b, alpha, tm, tf):
    n, d = x.shape
    dff = w1.shape[1]
    return pl.pallas_call(
        functools.partial(_ffn_kernel, alpha=alpha),
        grid=(n // tm, dff // tf),
        in_specs=[pl.BlockSpec((tm, d), lambda i, f: (i, 0)),
                  pl.BlockSpec((d, tf), lambda i, f: (0, f)),
                  pl.BlockSpec((d, tf), lambda i, f: (0, f)),
                  pl.BlockSpec((tf, d), lambda i, f: (f, 0)),
                  pl.BlockSpec((1, d), lambda i, f: (0, 0)),
                  pl.BlockSpec((1, d), lambda i, f: (0, 0))],
        out_specs=pl.BlockSpec((tm, d), lambda i, f: (i, 0)),
        out_shape=jax.ShapeDtypeStruct((n, d), F32),
        scratch_shapes=[pltpu.VMEM((tm, d), F32), pltpu.VMEM((tm, d), BF16)],
        compiler_params=_cparams(("parallel", "arbitrary")),
        name="ffn_dense",
    )(x, w1, w3, w2, g.reshape(1, d), b.reshape(1, d))


def _moe_kernel(x_ref, gate_ref, w1_ref, w3_ref, w2_ref, g_ref, b_ref, o_ref,
                acc_scr, xb_scr, *, alpha):
    e = pl.program_id(1)
    f = pl.program_id(2)

    @pl.when((e == 0) & (f == 0))
    def _():
        acc_scr[...] = jnp.zeros_like(acc_scr)
        xb_scr[...] = x_ref[...].astype(BF16)

    gate = gate_ref[...]
    lane = lax.broadcasted_iota(jnp.int32, gate.shape, 1)
    ge = jnp.sum(jnp.where(lane == e, gate, 0.0), -1, keepdims=True)
    xb = xb_scr[...]
    h1 = jnp.dot(xb, w1_ref[...].astype(BF16), preferred_element_type=F32)
    h3 = jnp.dot(xb, w3_ref[...].astype(BF16), preferred_element_type=F32)
    hid = (_silu(h1) * h3).astype(BF16)
    acc_scr[...] += ge * jnp.dot(hid, w2_ref[...].astype(BF16), preferred_element_type=F32)

    @pl.when((e == pl.num_programs(1) - 1) & (f == pl.num_programs(2) - 1))
    def _():
        o_ref[...] = _layer_norm(alpha * x_ref[...] + acc_scr[...], g_ref[...], b_ref[...])


def _ffn_moe(x, gate, w1, w3, w2, g, b, alpha, tm, tf):
    n, d = x.shape
    n_exp, _, dff = w1.shape
    return pl.pallas_call(
        functools.partial(_moe_kernel, alpha=alpha),
        grid=(n // tm, n_exp, dff // tf),
        in_specs=[pl.BlockSpec((tm, d), lambda i, e, f: (i, 0)),
                  pl.BlockSpec((tm, LANES), lambda i, e, f: (i, 0)),
                  pl.BlockSpec((None, d, tf), lambda i, e, f: (e, 0, f)),
                  pl.BlockSpec((None, d, tf), lambda i, e, f: (e, 0, f)),
                  pl.BlockSpec((None, tf, d), lambda i, e, f: (e, f, 0)),
                  pl.BlockSpec((1, d), lambda i, e, f: (0, 0)),
                  pl.BlockSpec((1, d), lambda i, e, f: (0, 0))],
        out_specs=pl.BlockSpec((tm, d), lambda i, e, f: (i, 0)),
        out_shape=jax.ShapeDtypeStruct((n, d), F32),
        scratch_shapes=[pltpu.VMEM((tm, d), F32), pltpu.VMEM((tm, d), BF16)],
        compiler_params=_cparams(("parallel", "arbitrary", "arbitrary")),
        name="ffn_moe",
    )(x, gate, w1, w3, w2, g.reshape(1, d), b.reshape(1, d))


def _row_tile(n, cap):
    t = cap
    while n % t:
        t //= 2
    return t


def _ff_tile(dff):
    for t in (512, 256, 128):
        if dff % t == 0:
            return t
    return dff


def kernel(x_prompt, x_sample, cache_k, cache_v, page_table, state_a, state_a_conv, state_c, ln_in_g, ln_in_b, w_in, w_out, a_conv_w, a_a_log, a_dt_bias, a_norm_w, c_lb_logits, c_norm_w, ln1_g, ln1_b, ln2_g, ln2_b, ffn_w1, ffn_w3, ffn_w2, moe_router, moe_w1, moe_w3, moe_w2):
    bp, tp, d = x_prompt.shape
    bs, ts, _ = x_sample.shape
    depth = w_in.shape[0]
    D = HEAD_DIM
    wa, wb = d // 4, d // 2
    wc = d - wa - wb
    ha, hb, hc = wa // D, wb // D, wc // D
    kc = hc * D
    alpha = (2.0 * depth) ** 0.25
    off_b = 4 * wa + 2 * ha
    off_c = off_b + 3 * wb
    p_in = off_c + 2 * kc + 2 * wc
    assert w_in.shape[2] == p_in and 2 * ha <= LANES

    seg_w = [4 * wa, wb, wb, wb, 2 * kc + 2 * wc, LANES]
    segs, lo = [], 0
    for sw in seg_w:
        segs.append((lo, lo + sw))
        lo += sw

    lb_all = jnp.cumsum(jax.nn.softmax(c_lb_logits.astype(F32), axis=0), axis=0)
    lb_all = lb_all - lb_all[0:1]

    n_pool, page = cache_k.shape[1], cache_k.shape[2]
    ck = cache_k.reshape(depth, n_pool, page, wb)
    cv = cache_v.reshape(depth, n_pool, page, wb)

    np_rows, ns_rows = bp * tp, bs * ts
    tm_p = _row_tile(np_rows, 256)
    tm_s = _row_tile(ns_rows, 256)
    xp = _ln_rows(x_prompt.reshape(np_rows, d), ln_in_g, ln_in_b, tm_p)
    xs = _ln_rows(x_sample.reshape(ns_rows, d), ln_in_g, ln_in_b, tm_s)

    outs = {k: [] for k in ("kp", "vp", "ks", "vs", "sap", "sas", "cvp", "cvs", "scp", "scs")}
    for l in range(depth):
        wl = w_in[l]
        w_perm = jnp.concatenate(
            [wl[:, :4 * wa], wl[:, off_b:], wl[:, 4 * wa:off_b],
             jnp.zeros((d, LANES - 2 * ha), wl.dtype)], axis=1).astype(BF16)
        w_o = w_out[l].astype(BF16)
        cw = a_conv_w[l]

        h_a, q_b, k_b, v_b, h_c, h_ab = _in_proj(xp, w_perm, segs, tm_p)
        h_a3 = h_a.reshape(bp, tp, 4 * wa)
        oa, sa_new = _mixa_prompt(h_a3, h_ab.reshape(bp, tp, LANES), cw, a_a_log[l], a_dt_bias[l],
                                  a_norm_w[l], ha)
        ob = _moba_prompt(q_b.reshape(bp, tp, wb), k_b.reshape(bp, tp, wb), v_b.reshape(bp, tp, wb), hb)
        oc, sc_new = _hgrn_prompt(h_c.reshape(bp, tp, 4 * wc), lb_all[l], c_norm_w[l], hc)
        outs["kp"].append(k_b.reshape(bp, tp, hb, D))
        outs["vp"].append(v_b.reshape(bp, tp, hb, D))
        outs["sap"].append(sa_new)
        conv_full = jnp.concatenate([jnp.zeros((bp, CONV_W - 1, 3 * wa), F32), h_a3[:, :, :3 * wa]], axis=1) \
            if tp < CONV_W - 1 else h_a3[:, :, :3 * wa]
        outs["cvp"].append(conv_full[:, conv_full.shape[1] - (CONV_W - 1):])
        outs["scp"].append(sc_new)
        router = moe_router[l // 2] if l % 2 else None
        x1p, gate_p = _out_proj(xp, oa.reshape(np_rows, wa), ob.reshape(np_rows, wb),
                                oc.reshape(np_rows, wc), w_o, ln1_g[l], ln1_b[l], router, alpha, tm_p)

        h_a, q_b, k_b, v_b, h_c, h_ab = _in_proj(xs, w_perm, segs, tm_s)
        h_a3 = h_a.reshape(bs, ts, 4 * wa)
        x_full = jnp.concatenate([state_a_conv[l].astype(F32), h_a3[:, :, :3 * wa]], axis=1)
        x_t = x_full.transpose(1, 2, 0)
        z_t = h_a3[:, :, 3 * wa:].transpose(1, 2, 0)
        ab_t = h_ab.reshape(bs, ts, LANES)[:, :, :max(8, 2 * ha)].transpose(1, 2, 0)
        s0_t = state_a[l].astype(F32).transpose(1, 2, 3, 0)
        oa_t, sa_t = _mixa_sample(x_t, z_t, ab_t, cw.reshape(CONV_W, 3 * wa, 1), a_a_log[l], a_dt_bias[l],
                                  a_norm_w[l], s0_t, ha, ts)
        oa = oa_t.transpose(2, 0, 1).reshape(ns_rows, wa)
        outs["sas"].append(sa_t.transpose(3, 0, 1, 2))
        outs["cvs"].append(x_full[:, x_full.shape[1] - (CONV_W - 1):])
        ob = _moba_sample(q_b.reshape(bs, ts, wb), k_b.reshape(bs, ts, wb), v_b.reshape(bs, ts, wb),
                          ck, cv, page_table, l, hb)
        outs["ks"].append(k_b.reshape(bs, ts, hb, D))
        outs["vs"].append(v_b.reshape(bs, ts, hb, D))
        hc_t = h_c.reshape(bs, ts, 4 * wc).transpose(1, 2, 0)
        sc0_t = state_c[l].astype(F32).transpose(1, 2, 3, 0)
        oc_t, sc_t = _hgrn_sample(hc_t, lb_all[l], c_norm_w[l], sc0_t, hc, ts)
        oc = oc_t.transpose(2, 0, 1).reshape(ns_rows, wc)
        outs["scs"].append(sc_t.transpose(3, 0, 1, 2))
        x1s, gate_s = _out_proj(xs, oa, ob.reshape(ns_rows, wb), oc, w_o, ln1_g[l], ln1_b[l], router,
                                alpha, tm_s)

        if l % 2 == 0:
            w1, w3, w2 = ffn_w1[l // 2].astype(BF16), ffn_w3[l // 2].astype(BF16), ffn_w2[l // 2].astype(BF16)
            tf = _ff_tile(w1.shape[1])
            xp = _ffn_dense(x1p, w1, w3, w2, ln2_g[l], ln2_b[l], alpha, _row_tile(np_rows, 1024), tf)
            xs = _ffn_dense(x1s, w1, w3, w2, ln2_g[l], ln2_b[l], alpha, _row_tile(ns_rows, 1024), tf)
        else:
            w1, w3, w2 = moe_w1[l // 2], moe_w3[l // 2], moe_w2[l // 2]
            tf = _ff_tile(w1.shape[2])
            xp = _ffn_moe(x1p, gate_p, w1, w3, w2, ln2_g[l], ln2_b[l], alpha, _row_tile(np_rows, 1024), tf)
            xs = _ffn_moe(x1s, gate_s, w1, w3, w2, ln2_g[l], ln2_b[l], alpha, _row_tile(ns_rows, 1024), tf)

    st = lambda key: jnp.stack(outs[key])
    return (xp.reshape(bp, tp, d), xs.reshape(bs, ts, d), st("kp"), st("vp"), st("ks"), st("vs"),
            st("sap"), st("sas"), st("cvp"), st("cvs"), st("scp"), st("scs"))
```

```python
import functools
import math

import numpy as np
import jax
import jax.numpy as jnp
from jax import lax
from jax.experimental import pallas as pl
from jax.experimental.pallas import tpu as pltpu

HEAD_DIM = 64
CONV_W = 4
CHUNK = 64
MOBA_BLOCK = 256
MOBA_TOPK = 3
MOE_TOPK = 2
LN_EPS = 1e-5
RMS_EPS = 1e-6
LANES = 128
VMEM_LIMIT = 56 * 1024 * 1024
HI = lax.Precision.HIGHEST
F32 = jnp.float32
BF16 = jnp.bfloat16


def _cparams(sem):
    return pltpu.CompilerParams(dimension_semantics=sem, vmem_limit_bytes=VMEM_LIMIT)


def _bdot(a, b):
    return jnp.dot(a.astype(BF16), b.astype(BF16), preferred_element_type=F32)


def _split2(a):
    hi = a.astype(BF16)
    return hi, (a - hi.astype(F32)).astype(BF16)


def _es(spec, a, b):
    ah, al = _split2(a)
    bh, bl = _split2(b)
    e = functools.partial(jnp.einsum, spec, preferred_element_type=F32)
    return e(ah, bh) + (e(ah, bl) + e(al, bh))


def _es_mask(spec, m, b):
    bh = b.astype(BF16)
    r = b - bh.astype(F32)
    bm = r.astype(BF16)
    bl = (r - bm.astype(F32)).astype(BF16)
    mb = m.astype(BF16)
    e = functools.partial(jnp.einsum, spec, preferred_element_type=F32)
    return e(mb, bh) + (e(mb, bm) + e(mb, bl))


def _layer_norm(x, g, b):
    mu = jnp.mean(x, -1, keepdims=True)
    xc = x - mu
    var = jnp.mean(xc * xc, -1, keepdims=True)
    return xc * lax.rsqrt(var + LN_EPS) * g + b


def _sigmoid(x):
    return 1.0 / (1.0 + jnp.exp(-x))


def _silu(x):
    return x * _sigmoid(x)


def _softplus(x):
    return jnp.maximum(x, 0.0) + jnp.log1p(jnp.exp(-jnp.abs(x)))


def _ln_kernel(x_ref, g_ref, b_ref, o_ref):
    o_ref[...] = _layer_norm(x_ref[...], g_ref[...], b_ref[...])


def _ln_rows(x, g, b, tm):
    n, d = x.shape
    return pl.pallas_call(
        _ln_kernel,
        grid=(n // tm,),
        in_specs=[pl.BlockSpec((tm, d), lambda i: (i, 0)),
                  pl.BlockSpec((1, d), lambda i: (0, 0)),
                  pl.BlockSpec((1, d), lambda i: (0, 0))],
        out_specs=pl.BlockSpec((tm, d), lambda i: (i, 0)),
        out_shape=jax.ShapeDtypeStruct((n, d), F32),
        compiler_params=_cparams(("parallel",)),
        name="ln_rows",
    )(x, g.reshape(1, d), b.reshape(1, d))


def _in_proj_kernel(x_ref, w_ref, *o_refs, segs):
    xb = x_ref[...].astype(BF16)
    for o_ref, (lo, hi) in zip(o_refs, segs):
        o_ref[...] = jnp.dot(xb, w_ref[:, lo:hi], preferred_element_type=F32)


def _in_proj(x, w, segs, tm):
    n, d = x.shape
    return pl.pallas_call(
        functools.partial(_in_proj_kernel, segs=segs),
        grid=(n // tm,),
        in_specs=[pl.BlockSpec((tm, d), lambda i: (i, 0)),
                  pl.BlockSpec(w.shape, lambda i: (0, 0))],
        out_specs=[pl.BlockSpec((tm, hi - lo), lambda i: (i, 0)) for lo, hi in segs],
        out_shape=[jax.ShapeDtypeStruct((n, hi - lo), F32) for lo, hi in segs],
        compiler_params=_cparams(("parallel",)),
        name="in_proj",
    )(x, w)


def _heads(a, nh):
    nb, c, _ = a.shape
    d = HEAD_DIM
    st = jnp.stack([a[:, :, h * d:(h + 1) * d] for h in range(nh)], axis=1)
    return st.reshape(nb * nh, c, d)


def _unheads(o, nb, nh):
    _, c, d = o.shape
    o4 = o.reshape(nb, nh, c, d)
    return jnp.concatenate([o4[:, h] for h in range(nh)], axis=-1)


def _gated_rms(o, w, z):
    o = o * lax.rsqrt(jnp.mean(o * o, -1, keepdims=True) + RMS_EPS)
    return o * w * _silu(z)


def _mixa_prompt_kernel(h_ref, ab_ref, cw_ref, al_ref, dtb_ref, nw_ref,
                        o_ref, sa_ref, s_scr, prev_scr, *, nb, nh):
    c = pl.program_id(0)
    C, D = CHUNK, HEAD_DIM
    W = nh * D
    G = nb * nh

    @pl.when(c == 0)
    def _():
        s_scr[...] = jnp.zeros_like(s_scr)
        prev_scr[...] = jnp.zeros_like(prev_scr)

    x = h_ref[:, :, 0:3 * W]
    z = h_ref[:, :, 3 * W:4 * W]
    xp = jnp.concatenate([prev_scr[...], x], axis=1)
    cw = cw_ref[...]
    y = (xp[:, 5:5 + C] * cw[0] + xp[:, 6:6 + C] * cw[1]
         + xp[:, 7:7 + C] * cw[2] + xp[:, 8:8 + C] * cw[3])
    prev_scr[...] = x[:, C - 8:C]
    y = _silu(y)

    q = _heads(y[:, :, 0:W], nh)
    k = _heads(y[:, :, W:2 * W], nh)
    v = _heads(y[:, :, 2 * W:3 * W], nh)
    zh = _heads(z, nh)
    q = q * lax.rsqrt(jnp.sum(q * q, -1, keepdims=True) + 1e-6) * (D ** -0.5)
    k = k * lax.rsqrt(jnp.sum(k * k, -1, keepdims=True) + 1e-6)

    ab = ab_ref[...]
    dt = _softplus(ab + dtb_ref[...])
    gfull = -jnp.exp(al_ref[...]) * dt
    bfull = _sigmoid(ab)

    def lane_bc(a, off):
        st = jnp.stack([jnp.broadcast_to(a[:, :, off + h:off + h + 1], (nb, C, D))
                        for h in range(nh)], axis=1)
        return st.reshape(G, C, D)

    gb = lane_bc(gfull, 0)
    beta = lane_bc(bfull, nh)

    ii = lax.broadcasted_iota(jnp.int32, (C, C), 0)
    jj = lax.broadcasted_iota(jnp.int32, (C, C), 1)

    def bcg(m):
        return jnp.broadcast_to(m.astype(F32), (G, C, C))

    gam = _es_mask('gij,gjk->gik', bcg(ii >= jj), gb)
    gam_t = jnp.einsum('gik,gjk->gij', bcg(jj == 0), gam, precision=HI,
                       preferred_element_type=F32)
    dec = jnp.exp(jnp.where(ii >= jj, gam - gam_t, -jnp.inf))

    kb = k * beta
    nmat = jnp.where(ii > jj, _es('gik,gjk->gij', kb, k) * dec, 0.0)
    xinv = (ii == jj).astype(F32) - jnp.where(ii // 2 == jj // 2, nmat, 0.0)
    s = 4
    while s <= C:
        off = jnp.where((ii // s == jj // s) & (ii // (s // 2) != jj // (s // 2)), nmat, 0.0)
        xinv = xinv - _es('gij,gjk->gik', xinv, _es('gij,gjk->gik', off, xinv))
        s *= 2

    egam = jnp.exp(gam)
    rhs = jnp.concatenate([v * beta, kb * egam], axis=-1)
    sol = _es('gij,gjk->gik', xinv, rhs)
    st = s_scr[...]
    u = sol[:, :, 0:D] - _es('gck,gkv->gcv', sol[:, :, D:2 * D], st)
    attn = _es('gik,gjk->gij', q, k) * dec
    o = _es('gck,gkv->gcv', q * egam, st) + _es('gij,gjv->giv', attn, u)
    glast = gam[:, C - 1:C, :]
    kdec = k * jnp.exp(glast - gam)
    st_new = st * jnp.exp(glast) + _es('gck,gcv->gkv', kdec, u)
    s_scr[...] = st_new

    o = _gated_rms(o, nw_ref[...], zh)
    o_ref[...] = _unheads(o, nb, nh)

    @pl.when(c == pl.num_programs(0) - 1)
    def _():
        sa_ref[...] = st_new.reshape(nb, nh, D, D)


def _mixa_prompt(h_a, h_ab, cw, a_log, dt_bias, norm_w, nh):
    nb, t, wa4 = h_a.shape
    C, D = CHUNK, HEAD_DIM
    W = nh * D
    pad = LANES - nh
    al = jnp.pad(a_log, (0, pad)).reshape(1, LANES)
    dtb = jnp.pad(dt_bias, (0, pad)).reshape(1, LANES)
    return pl.pallas_call(
        functools.partial(_mixa_prompt_kernel, nb=nb, nh=nh),
        grid=(t // C,),
        in_specs=[pl.BlockSpec((nb, C, wa4), lambda c: (0, c, 0)),
                  pl.BlockSpec((nb, C, LANES), lambda c: (0, c, 0)),
                  pl.BlockSpec((CONV_W, 3 * W), lambda c: (0, 0)),
                  pl.BlockSpec((1, LANES), lambda c: (0, 0)),
                  pl.BlockSpec((1, LANES), lambda c: (0, 0)),
                  pl.BlockSpec((1, D), lambda c: (0, 0))],
        out_specs=[pl.BlockSpec((nb, C, W), lambda c: (0, c, 0)),
                   pl.BlockSpec((nb, nh, D, D), lambda c: (0, 0, 0, 0))],
        out_shape=[jax.ShapeDtypeStruct((nb, t, W), F32),
                   jax.ShapeDtypeStruct((nb, nh, D, D), F32)],
        scratch_shapes=[pltpu.VMEM((nb * nh, D, D), F32),
                        pltpu.VMEM((nb, 8, 3 * W), F32)],
        compiler_params=_cparams(("arbitrary",)),
        name="mixa_prompt",
    )(h_a, h_ab, cw, al, dtb, norm_w.reshape(1, D))


def _hgrn_prompt_kernel(h_ref, loglb_ref, log1mlb_ref, omlb_ref, nw_ref,
                        o_ref, sc_ref, st_scr, *, nb, nh):
    c = pl.program_id(0)
    C, D = CHUNK, HEAD_DIM
    W = nh * D
    G = nb * nh

    @pl.when(c == 0)
    def _():
        st_scr[...] = jnp.zeros_like(st_scr)

    x = h_ref[...]
    qx = _heads(x[:, :, 0:W], nh)
    fx = _heads(x[:, :, W:2 * W], nh)
    iv = _heads(x[:, :, 2 * W:3 * W], nh)
    zh = _heads(x[:, :, 3 * W:4 * W], nh)

    la = loglb_ref[...]
    lbb = log1mlb_ref[...] - _softplus(-fx)
    mx = jnp.maximum(la, lbb)
    logf = mx + jnp.log1p(jnp.exp(-jnp.abs(la - lbb)))
    kc = omlb_ref[...] * _sigmoid(-fx)
    q = _silu(qx)

    ii = lax.broadcasted_iota(jnp.int32, (C, C), 0)
    jj = lax.broadcasted_iota(jnp.int32, (C, C), 1)
    ri = lax.broadcasted_iota(jnp.int32, (C, D), 0)
    gam = _es_mask('gij,gjk->gik', jnp.broadcast_to((ii >= jj).astype(F32), (G, C, C)), logf)

    def block_starts(hb):
        g4 = gam.reshape(G, C // hb, hb, D)
        starts = g4[:, :, 0:1, :]
        nxt = jnp.concatenate([starts[:, 1:], starts[:, -1:]], axis=1)
        own = jnp.broadcast_to(starts, g4.shape).reshape(G, C, D)
        nx = jnp.broadcast_to(nxt, g4.shape).reshape(G, C, D)
        return own, nx

    attn = jnp.zeros((G, C, C), F32)
    hb = C // 2
    while hb >= 8:
        s = 2 * hb
        own, nx = block_starts(hb)
        upper = (ri % s) >= hb
        qs = jnp.where(upper, q * jnp.exp(gam - own), 0.0)
        ks = jnp.where(upper, 0.0, kc * jnp.exp(jnp.where(upper, 0.0, nx - gam)))
        attn = attn + jnp.where(ii // s == jj // s, _es('gik,gjk->gij', qs, ks), 0.0)
        hb //= 2
    own, _ = block_starts(8)
    qd = q * jnp.exp(gam - own)
    kd = kc * jnp.exp(own - gam)
    attn = attn + jnp.where((ii // 8 == jj // 8) & (ii >= jj), _es('gik,gjk->gij', qd, kd), 0.0)

    st = st_scr[...]
    o = _es('gck,gvk->gcv', q * jnp.exp(gam), st) + _es('gij,gjv->giv', attn, iv)
    glast = gam[:, C - 1:C, :]
    kdec = kc * jnp.exp(glast - gam)
    st_new = st * jnp.exp(glast) + _es('gcv,gck->gvk', iv, kdec)
    st_scr[...] = st_new

    o = _gated_rms(o, nw_ref[...], zh)
    o_ref[...] = _unheads(o, nb, nh)

    @pl.when(c == pl.num_programs(0) - 1)
    def _():
        di = lax.broadcasted_iota(jnp.int32, (D, D), 0)
        dj = lax.broadcasted_iota(jnp.int32, (D, D), 1)
        eye = jnp.broadcast_to((di == dj).astype(F32), (G, D, D))
        sc_ref[...] = _es_mask('gkj,gvj->gkv', eye, st_new).reshape(nb, nh, D, D)


def _hgrn_prompt(h_c, lb, norm_w, nh):
    nb, t, wc4 = h_c.shape
    C, D = CHUNK, HEAD_DIM
    W = nh * D
    G = nb * nh

    def per_group(a):
        return jnp.tile(a.reshape(nh, 1, D), (nb, 1, 1))

    return pl.pallas_call(
        functools.partial(_hgrn_prompt_kernel, nb=nb, nh=nh),
        grid=(t // C,),
        in_specs=[pl.BlockSpec((nb, C, wc4), lambda c: (0, c, 0)),
                  pl.BlockSpec((G, 1, D), lambda c: (0, 0, 0)),
                  pl.BlockSpec((G, 1, D), lambda c: (0, 0, 0)),
                  pl.BlockSpec((G, 1, D), lambda c: (0, 0, 0)),
                  pl.BlockSpec((1, D), lambda c: (0, 0))],
        out_specs=[pl.BlockSpec((nb, C, W), lambda c: (0, c, 0)),
                   pl.BlockSpec((nb, nh, D, D), lambda c: (0, 0, 0, 0))],
        out_shape=[jax.ShapeDtypeStruct((nb, t, W), F32),
                   jax.ShapeDtypeStruct((nb, nh, D, D), F32)],
        scratch_shapes=[pltpu.VMEM((G, D, D), F32)],
        compiler_params=_cparams(("arbitrary",)),
        name="hgrn_prompt",
    )(h_c, per_group(jnp.log(lb)), per_group(jnp.log1p(-lb)), per_group(1.0 - lb),
      norm_w.reshape(1, D))


def _mixa_sample_kernel(al_ref, dtb_ref, xq_ref, xk_ref, xv_ref, z_ref, ab_ref,
                        cwq_ref, cwk_ref, cwv_ref, nw_ref, s0_ref,
                        o_ref, s_ref, kq_scr, *, nh, ts):
    h = pl.program_id(0)
    D = HEAD_DIM
    s_ref[...] = s0_ref[...]
    neg_a = -jnp.exp(al_ref[h])
    dtb = dtb_ref[h]

    def conv(x_ref, cw_ref, t):
        acc = x_ref[t] * cw_ref[0]
        for j in range(1, CONV_W):
            acc = acc + x_ref[t + j] * cw_ref[j]
        return _silu(acc)

    for t in range(ts):
        q = conv(xq_ref, cwq_ref, t)
        k = conv(xk_ref, cwk_ref, t)
        v = conv(xv_ref, cwv_ref, t)
        q = q * lax.rsqrt(jnp.sum(q * q, 0, keepdims=True) + 1e-6) * (D ** -0.5)
        k = k * lax.rsqrt(jnp.sum(k * k, 0, keepdims=True) + 1e-6)
        a = ab_ref[t, pl.ds(h, 1), :]
        b = ab_ref[t, pl.ds(nh + h, 1), :]
        decay = jnp.exp(neg_a * _softplus(a + dtb))
        beta = _sigmoid(b)
        kq_scr[0] = k
        kq_scr[1] = q

        def pass1(d, r):
            kd = kq_scr[0, pl.ds(d, 1), :]
            return r + (s_ref[0, d] * decay) * kd

        r = lax.fori_loop(0, D, pass1, jnp.zeros_like(v))
        vn = beta * (v - r)

        def pass2(d, o):
            kd = kq_scr[0, pl.ds(d, 1), :]
            qd = kq_scr[1, pl.ds(d, 1), :]
            sd = s_ref[0, d] * decay + kd * vn
            s_ref[0, d] = sd
            return o + sd * qd

        o = lax.fori_loop(0, D, pass2, jnp.zeros_like(v))
        o = o * lax.rsqrt(jnp.mean(o * o, 0, keepdims=True) + RMS_EPS)
        o_ref[t] = o * nw_ref[...] * _silu(z_ref[t])


def _mixa_sample(x_t, z_t, ab_t, cw_t, a_log, dt_bias, norm_w, s0_t, nh, ts):
    D = HEAD_DIM
    W = nh * D
    nb = x_t.shape[-1]
    tt = x_t.shape[0]
    nab = ab_t.shape[1]
    smem = pl.BlockSpec(memory_space=pltpu.SMEM)
    return pl.pallas_call(
        functools.partial(_mixa_sample_kernel, nh=nh, ts=ts),
        grid=(nh,),
        in_specs=[smem, smem,
                  pl.BlockSpec((tt, D, nb), lambda h: (0, h, 0)),
                  pl.BlockSpec((tt, D, nb), lambda h: (0, nh + h, 0)),
                  pl.BlockSpec((tt, D, nb), lambda h: (0, 2 * nh + h, 0)),
                  pl.BlockSpec((ts, D, nb), lambda h: (0, h, 0)),
                  pl.BlockSpec((ts, nab, nb), lambda h: (0, 0, 0)),
                  pl.BlockSpec((CONV_W, D, 1), lambda h: (0, h, 0)),
                  pl.BlockSpec((CONV_W, D, 1), lambda h: (0, nh + h, 0)),
                  pl.BlockSpec((CONV_W, D, 1), lambda h: (0, 2 * nh + h, 0)),
                  pl.BlockSpec((D, 1), lambda h: (0, 0)),
                  pl.BlockSpec((1, D, D, nb), lambda h: (h, 0, 0, 0))],
        out_specs=[pl.BlockSpec((ts, D, nb), lambda h: (0, h, 0)),
                   pl.BlockSpec((1, D, D, nb), lambda h: (h, 0, 0, 0))],
        out_shape=[jax.ShapeDtypeStruct((ts, W, nb), F32),
                   jax.ShapeDtypeStruct((nh, D, D, nb), F32)],
        scratch_shapes=[pltpu.VMEM((2, D, nb), F32)],
        compiler_params=_cparams(("arbitrary",)),
        name="mixa_sample",
    )(a_log, dt_bias, x_t, x_t, x_t, z_t, ab_t, cw_t, cw_t, cw_t,
      norm_w.reshape(D, 1), s0_t)


def _hgrn_sample_kernel(q_ref, f_ref, i_ref, z_ref, lb_ref, nw_ref, s0_ref,
                        o_ref, s_ref, kq_scr, *, ts):
    D = HEAD_DIM
    s_ref[...] = s0_ref[...]
    lb = lb_ref[...]
    for t in range(ts):
        fx = f_ref[t]
        f = lb + (1.0 - lb) * _sigmoid(fx)
        kq_scr[0] = f
        kq_scr[1] = (1.0 - lb) * _sigmoid(-fx)
        kq_scr[2] = _silu(q_ref[t])
        v = i_ref[t]

        def body(d, o):
            fd = kq_scr[0, pl.ds(d, 1), :]
            kd = kq_scr[1, pl.ds(d, 1), :]
            qd = kq_scr[2, pl.ds(d, 1), :]
            sd = s_ref[0, d] * fd + kd * v
            s_ref[0, d] = sd
            return o + sd * qd

        o = lax.fori_loop(0, D, body, jnp.zeros_like(v))
        o = o * lax.rsqrt(jnp.mean(o * o, 0, keepdims=True) + RMS_EPS)
        o_ref[t] = o * nw_ref[...] * _silu(z_ref[t])


def _hgrn_sample(hc_t, lb, norm_w, s0_t, nh, ts):
    D = HEAD_DIM
    W = nh * D
    nb = hc_t.shape[-1]
    return pl.pallas_call(
        functools.partial(_hgrn_sample_kernel, ts=ts),
        grid=(nh,),
        in_specs=[pl.BlockSpec((ts, D, nb), lambda h: (0, h, 0)),
                  pl.BlockSpec((ts, D, nb), lambda h: (0, nh + h, 0)),
                  pl.BlockSpec((ts, D, nb), lambda h: (0, 2 * nh + h, 0)),
                  pl.BlockSpec((ts, D, nb), lambda h: (0, 3 * nh + h, 0)),
                  pl.BlockSpec((D, 1), lambda h: (h, 0)),
                  pl.BlockSpec((D, 1), lambda h: (0, 0)),
                  pl.BlockSpec((1, D, D, nb), lambda h: (h, 0, 0, 0))],
        out_specs=[pl.BlockSpec((ts, D, nb), lambda h: (0, h, 0)),
                   pl.BlockSpec((1, D, D, nb), lambda h: (h, 0, 0, 0))],
        out_shape=[jax.ShapeDtypeStruct((ts, W, nb), F32),
                   jax.ShapeDtypeStruct((nh, D, D, nb), F32)],
        scratch_shapes=[pltpu.VMEM((3, D, nb), F32)],
        compiler_params=_cparams(("arbitrary",)),
        name="hgrn_sample",
    )(hc_t, hc_t, hc_t, hc_t, lb.reshape(W, 1), norm_w.reshape(D, 1), s0_t)


def _block_mean_kernel(k_ref, o_ref):
    o_ref[...] = jnp.mean(k_ref[...], axis=0, keepdims=True)


def _block_mean(k):
    nb, t, w = k.shape
    nblk = t // MOBA_BLOCK
    out = pl.pallas_call(
        _block_mean_kernel,
        grid=(nb, nblk),
        in_specs=[pl.BlockSpec((None, MOBA_BLOCK, w), lambda b, n: (b, n, 0))],
        out_specs=pl.BlockSpec((None, None, 1, w), lambda b, n: (b, n, 0, 0)),
        out_shape=jax.ShapeDtypeStruct((nb, nblk, 1, w), F32),
        compiler_params=_cparams(("parallel", "parallel")),
        name="moba_block_mean",
    )(k)
    return out.reshape(nb, nblk, w)


def _alibi_slope(h, nh):
    return float(2.0 ** (-8.0 * (h + 1) / nh))


LOG2E = 1.4426950408889634


def _moba_prompt_kernel(qi_ref, kj_ref, q_ref, k_ref, v_ref, km_ref, o_ref,
                        acc_scr, m_scr, l_scr, sel_scr, qs_scr, bias_scr, *, nh, nblk):
    step = pl.program_id(1)
    qi = qi_ref[step]
    kj = kj_ref[step]
    BS, D = MOBA_BLOCK, HEAD_DIM
    kpos = lax.broadcasted_iota(jnp.int32, (BS, BS), 0)
    qpos = lax.broadcasted_iota(jnp.int32, (BS, BS), 1)

    @pl.when(step == 0)
    def _():
        rel0 = (qpos - kpos).astype(F32)
        for h in range(nh):
            bias_scr[h] = (_alibi_slope(h, nh) * LOG2E) * rel0

    @pl.when(kj == 0)
    def _():
        m_scr[...] = jnp.full_like(m_scr, -jnp.inf)
        l_scr[...] = jnp.zeros_like(l_scr)
        acc_scr[...] = jnp.zeros_like(acc_scr)
        nn = lax.broadcasted_iota(jnp.int32, (nblk, BS), 0)
        elig = nn < qi
        for h in range(nh):
            qh = q_ref[:, h * D:(h + 1) * D]
            qs_scr[h] = (qh * (D ** -0.5 * LOG2E)).astype(BF16)
            gate = _es('nd,qd->nq', km_ref[:, h * D:(h + 1) * D], qh)
            gate = jnp.where(elig, gate, -jnp.inf)
            rank = jnp.zeros((nblk, BS), F32)
            for m in range(nblk):
                gm = gate[m:m + 1, :]
                ahead = (gm > gate) | ((gm == gate) & (nn > m))
                rank = rank + jnp.where(ahead, 1.0, 0.0)
            sel_scr[h] = jnp.where(elig & (rank < MOBA_TOPK), 0.0, -jnp.inf)

    def attend(mask_fn):
        hs = range(nh)
        raw = [jnp.einsum('kd,qd->kq', k_ref[:, h * D:(h + 1) * D].astype(BF16), qs_scr[h],
                          preferred_element_type=F32) for h in hs]
        st = [mask_fn(h, raw[h] - bias_scr[h]) for h in hs]
        m_old = m_scr[...]
        l_old = l_scr[...]
        m_new = jnp.concatenate([jnp.max(st[h], axis=0, keepdims=True) for h in hs], axis=0)
        m_new = jnp.maximum(m_old, m_new)
        m_safe = jnp.where(m_new == -jnp.inf, 0.0, m_new)
        alpha = jnp.exp2(m_old - m_safe)
        p = [jnp.exp2(st[h] - m_safe[h:h + 1, :]) for h in hs]
        l_scr[...] = alpha * l_old + jnp.concatenate(
            [jnp.sum(p[h], axis=0, keepdims=True) for h in hs], axis=0)
        m_scr[...] = m_new
        pv = [jnp.einsum('kd,kq->dq', v_ref[:, h * D:(h + 1) * D].astype(BF16), p[h].astype(BF16),
                         preferred_element_type=F32) for h in hs]
        for h in hs:
            acc_scr[h] = alpha[h:h + 1, :] * acc_scr[h] + pv[h]

    @pl.when(kj < qi)
    def _():
        off = ((qi - kj) * BS).astype(F32)
        attend(lambda h, s: s + (sel_scr[h, pl.ds(kj, 1), :] - (_alibi_slope(h, nh) * LOG2E) * off))

    @pl.when(kj == qi)
    def _():
        attend(lambda h, s: jnp.where(qpos >= kpos, s, -jnp.inf))
        outs = [acc_scr[h] / l_scr[h:h + 1, :] for h in range(nh)]
        o_ref[...] = jnp.concatenate(outs, axis=0).T


def _moba_prompt(q, k, v, nh):
    nb, t, w = q.shape
    BS, D = MOBA_BLOCK, HEAD_DIM
    nblk = t // BS
    km = _block_mean(k)
    pairs = [(i, j) for i in range(nblk) for j in range(i + 1)]
    qi = jnp.asarray([p[0] for p in pairs], jnp.int32)
    kj = jnp.asarray([p[1] for p in pairs], jnp.int32)
    grid_spec = pltpu.PrefetchScalarGridSpec(
        num_scalar_prefetch=2,
        grid=(nb, len(pairs)),
        in_specs=[pl.BlockSpec((None, BS, w), lambda b, s, qi, kj: (b, qi[s], 0)),
                  pl.BlockSpec((None, BS, w), lambda b, s, qi, kj: (b, kj[s], 0)),
                  pl.BlockSpec((None, BS, w), lambda b, s, qi, kj: (b, kj[s], 0)),
                  pl.BlockSpec((None, nblk, w), lambda b, s, qi, kj: (b, 0, 0))],
        out_specs=pl.BlockSpec((None, BS, w), lambda b, s, qi, kj: (b, qi[s], 0)),
        scratch_shapes=[pltpu.VMEM((nh, D, BS), F32),
                        pltpu.VMEM((nh, BS), F32),
                        pltpu.VMEM((nh, BS), F32),
                        pltpu.VMEM((nh, nblk, BS), F32),
                        pltpu.VMEM((nh, BS, D), BF16),
                        pltpu.VMEM((nh, BS, BS), F32)])
    return pl.pallas_call(
        functools.partial(_moba_prompt_kernel, nh=nh, nblk=nblk),
        grid_spec=grid_spec,
        out_shape=jax.ShapeDtypeStruct((nb, t, w), F32),
        compiler_params=_cparams(("parallel", "arbitrary")),
        name="moba_prompt",
    )(qi, kj, q, k, v, km)


def _moba_sample_kernel(pt_ref, q_ref, kn_ref, vn_ref, *refs, nh, ts, past_len, bps):
    npg = 2 * bps
    k_refs, v_refs = refs[:npg], refs[npg:2 * npg]
    o_ref, m_scr, l_scr, g_scr, acc_scr = refs[2 * npg:]
    step = pl.program_id(1)
    BS, D = MOBA_BLOCK, HEAD_DIM
    page = BS // 2
    PR = page * nh
    R = ts * nh
    scale = D ** -0.5
    q = q_ref[...]
    qb = (q * scale).astype(BF16)

    r1 = lax.broadcasted_iota(jnp.int32, (R, 1), 0)
    tpos = (r1 // nh).astype(F32)
    hrow = r1 % nh
    slope = jnp.zeros((R, 1), F32)
    for h in range(nh):
        slope = jnp.where(hrow == h, _alibi_slope(h, nh), slope)
    lane = lax.broadcasted_iota(jnp.int32, (R, PR), 1)
    same_head = (lane % nh) == (lax.broadcasted_iota(jnp.int32, (R, PR), 0) % nh)
    ktok = (lane // nh).astype(F32)
    qpos = past_len + tpos

    def tile_rows(a):
        return jnp.concatenate([a] * ts, axis=0)

    pages = range(npg)
    kps = [k_refs[i][...] for i in pages]
    raw = [jnp.einsum('rd,kd->rk', qb, kps[i].astype(BF16), preferred_element_type=F32)
           for i in pages]
    scores = []
    for i in pages:
        kpos = ((step * bps + i // 2) * BS + (i % 2) * page).astype(F32) + ktok
        scores.append(jnp.where(same_head, raw[i] - slope * (qpos - kpos), -jnp.inf))
    m_blk = [jnp.maximum(jnp.max(scores[2 * b], axis=-1, keepdims=True),
                         jnp.max(scores[2 * b + 1], axis=-1, keepdims=True)) for b in range(bps)]
    p = [jnp.exp(scores[i] - m_blk[i // 2]) for i in pages]
    pv = [jnp.dot(p[i].astype(BF16), v_refs[i][...].astype(BF16), preferred_element_type=F32)
          for i in pages]
    ksum = [jnp.sum(kps[i].reshape(page, nh, D), axis=0) for i in pages]
    for b in range(bps):
        n = step * bps + b
        l_n = (jnp.sum(p[2 * b], axis=-1, keepdims=True)
               + jnp.sum(p[2 * b + 1], axis=-1, keepdims=True))
        kmean = (ksum[2 * b] + ksum[2 * b + 1]) * (1.0 / BS)
        gate = jnp.sum(q * tile_rows(kmean), axis=-1, keepdims=True)
        m_scr[n] = jnp.broadcast_to(m_blk[b], (R, LANES))
        l_scr[n] = jnp.broadcast_to(l_n, (R, LANES))
        g_scr[n] = jnp.broadcast_to(gate, (R, LANES))
        acc_scr[n] = pv[2 * b] + pv[2 * b + 1]

    @pl.when(step == pl.num_programs(1) - 1)
    def _():
        nb_past = past_len // BS
        gates = [g_scr[i][:, 0:1] for i in range(nb_past)]
        sels = []
        for i in range(nb_past):
            rank = jnp.zeros((R, 1), F32)
            for m in range(nb_past):
                if m == i:
                    continue
                ahead = (gates[m] > gates[i]) if m > i else (gates[m] >= gates[i])
                rank = rank + jnp.where(ahead, 1.0, 0.0)
            sels.append(rank < MOBA_TOPK)
        kn = kn_ref[...]
        vn = vn_ref[...]
        s_own = []
        for j in range(ts):
            sj = jnp.sum(q * tile_rows(kn[j * nh:(j + 1) * nh, :]), axis=-1, keepdims=True) * scale
            sj = sj - slope * (tpos - j)
            s_own.append(jnp.where(tpos >= j, sj, -jnp.inf))
        m_tot = s_own[0]
        for j in range(1, ts):
            m_tot = jnp.maximum(m_tot, s_own[j])
        for i in range(nb_past):
            m_tot = jnp.maximum(m_tot, jnp.where(sels[i], m_scr[i][:, 0:1], -jnp.inf))
        num = jnp.zeros((R, D), F32)
        den = jnp.zeros((R, 1), F32)
        for j in range(ts):
            pj = jnp.exp(s_own[j] - m_tot)
            num = num + pj * tile_rows(vn[j * nh:(j + 1) * nh, :])
            den = den + pj
        for i in range(nb_past):
            wi = jnp.where(sels[i], jnp.exp(m_scr[i][:, 0:1] - m_tot), 0.0)
            num = num + wi * acc_scr[i]
            den = den + wi * l_scr[i][:, 0:1]
        o_ref[...] = num / den


def _moba_sample(q, kn, vn, cache_k, cache_v, page_table, layer, nh):
    nb, R, D = q.shape
    ts = R // nh
    PR = cache_k.shape[2]
    page = PR // nh
    n_pages = page_table.shape[1]
    past_len = n_pages * page
    BS = MOBA_BLOCK
    assert BS == 2 * page and past_len % BS == 0 and ts <= BS
    nblk = past_len // BS
    bps = 2 if nblk % 2 == 0 else 1
    npg = 2 * bps
    pt = page_table.reshape(-1)

    def page_spec(i):
        return pl.BlockSpec((None, None, PR, D),
                            lambda b, s, pt: (layer, pt[b * n_pages + s * npg + i], 0, 0))

    tok = pl.BlockSpec((None, R, D), lambda b, s, pt: (b, 0, 0))
    pages = [page_spec(i) for i in range(npg)]
    grid_spec = pltpu.PrefetchScalarGridSpec(
        num_scalar_prefetch=1,
        grid=(nb, nblk // bps),
        in_specs=[tok, tok, tok] + pages + pages,
        out_specs=tok,
        scratch_shapes=[pltpu.VMEM((nblk, R, LANES), F32),
                        pltpu.VMEM((nblk, R, LANES), F32),
                        pltpu.VMEM((nblk, R, LANES), F32),
                        pltpu.VMEM((nblk, R, D), F32)])
    return pl.pallas_call(
        functools.partial(_moba_sample_kernel, nh=nh, ts=ts, past_len=past_len, bps=bps),
        grid_spec=grid_spec,
        out_shape=jax.ShapeDtypeStruct((nb, R, D), F32),
        compiler_params=_cparams(("parallel", "arbitrary")),
        name="moba_sample",
    )(pt, q, kn, vn, *([cache_k] * npg), *([cache_v] * npg))


def _router_gates(y, r, n_exp):
    logits = _es('mk,kn->mn', y, r)
    lane = lax.broadcasted_iota(jnp.int32, logits.shape, 1)
    logits = jnp.where(lane < n_exp, logits, -jnp.inf)
    m1 = jnp.max(logits, -1, keepdims=True)
    i1 = jnp.min(jnp.where(logits == m1, lane, LANES), -1, keepdims=True)
    first = lane == i1
    rest = jnp.where(first, -jnp.inf, logits)
    m2 = jnp.max(rest, -1, keepdims=True)
    i2 = jnp.min(jnp.where(rest == m2, lane, LANES), -1, keepdims=True)
    second = lane == i2
    e = jnp.exp(m2 - m1)
    den = 1.0 + e
    return jnp.where(first, 1.0 / den, 0.0) + jnp.where(second, e / den, 0.0)


def _out_proj_kernel(*refs, alpha, wa, wb, n_exp):
    if n_exp:
        x_ref, oa_ref, ob_ref, oc_ref, w_ref, g_ref, b_ref, r_ref, o_ref, gate_ref = refs
    else:
        x_ref, oa_ref, ob_ref, oc_ref, w_ref, g_ref, b_ref, o_ref = refs
    acc = jnp.dot(oa_ref[...].astype(BF16), w_ref[0:wa, :], preferred_element_type=F32)
    acc = acc + jnp.dot(ob_ref[...].astype(BF16), w_ref[wa:wa + wb, :], preferred_element_type=F32)
    acc = acc + jnp.dot(oc_ref[...].astype(BF16), w_ref[wa + wb:, :], preferred_element_type=F32)
    y = _layer_norm(alpha * x_ref[...] + acc, g_ref[...], b_ref[...])
    o_ref[...] = y
    if n_exp:
        gate_ref[...] = _router_gates(y, r_ref[...], n_exp)


def _out_proj(x, oa, ob, oc, w, g, b, router, alpha, tm):
    n, d = x.shape
    wa, wb, wc = oa.shape[1], ob.shape[1], oc.shape[1]
    n_exp = 0 if router is None else router.shape[1]
    row = lambda width: pl.BlockSpec((tm, width), lambda i: (i, 0))
    full = lambda shape: pl.BlockSpec(shape, lambda i: (0, 0))
    in_specs = [row(d), row(wa), row(wb), row(wc), full(w.shape), full((1, d)), full((1, d))]
    args = [x, oa, ob, oc, w, g.reshape(1, d), b.reshape(1, d)]
    out_specs = [row(d)]
    out_shape = [jax.ShapeDtypeStruct((n, d), F32)]
    if n_exp:
        in_specs.append(full((d, LANES)))
        args.append(jnp.pad(router, ((0, 0), (0, LANES - n_exp))))
        out_specs.append(row(LANES))
        out_shape.append(jax.ShapeDtypeStruct((n, LANES), F32))
    res = pl.pallas_call(
        functools.partial(_out_proj_kernel, alpha=alpha, wa=wa, wb=wb, n_exp=n_exp),
        grid=(n // tm,),
        in_specs=in_specs, out_specs=out_specs, out_shape=out_shape,
        compiler_params=_cparams(("parallel",)),
        name="out_proj_ln",
    )(*args)
    return (res[0], res[1]) if n_exp else (res[0], None)


def _ffn_kernel(x_ref, w1_ref, w3_ref, w2_ref, g_ref, b_ref, o_ref, acc_scr, xb_scr, *, alpha):
    f = pl.program_id(1)

    @pl.when(f == 0)
    def _():
        acc_scr[...] = jnp.zeros_like(acc_scr)
        xb_scr[...] = x_ref[...].astype(BF16)

    xb = xb_scr[...]
    h1 = jnp.dot(xb, w1_ref[...].astype(BF16), preferred_element_type=F32)
    h3 = jnp.dot(xb, w3_ref[...].astype(BF16), preferred_element_type=F32)
    hid = (_silu(h1) * h3).astype(BF16)
    acc_scr[...] += jnp.dot(hid, w2_ref[...].astype(BF16), preferred_element_type=F32)

    @pl.when(f == pl.num_programs(1) - 1)
    def _():
        o_ref[...] = _layer_norm(alpha * x_ref[...] + acc_scr[...], g_ref[...], b_ref[...])


def _ffn_dense(x, w1, w3, w2, g, b, alpha, tm, tf):
    n, d = x.shape
    dff = w1.shape[1]
    return pl.pallas_call(
        functools.partial(_ffn_kernel, alpha=alpha),
        grid=(n // tm, dff // tf),
        in_specs=[pl.BlockSpec((tm, d), lambda i, f: (i, 0)),
                  pl.BlockSpec((d, tf), lambda i, f: (0, f)),
                  pl.BlockSpec((d, tf), lambda i, f: (0, f)),
                  pl.BlockSpec((tf, d), lambda i, f: (f, 0)),
                  pl.BlockSpec((1, d), lambda i, f: (0, 0)),
                  pl.BlockSpec((1, d), lambda i, f: (0, 0))],
        out_specs=pl.BlockSpec((tm, d), lambda i, f: (i, 0)),
        out_shape=jax.ShapeDtypeStruct((n, d), F32),
        scratch_shapes=[pltpu.VMEM((tm, d), F32), pltpu.VMEM((tm, d), BF16)],
        compiler_params=_cparams(("parallel", "arbitrary")),
        name="ffn_dense",
    )(x, w1, w3, w2, g.reshape(1, d), b.reshape(1, d))


def _moe_kernel(x_ref, gate_ref, w1_ref, w3_ref, w2_ref, g_ref, b_ref, o_ref,
                acc_scr, xb_scr, *, alpha):
    e = pl.program_id(1)
    f = pl.program_id(2)

    @pl.when((e == 0) & (f == 0))
    def _():
        acc_scr[...] = jnp.zeros_like(acc_scr)
        xb_scr[...] = x_ref[...].astype(BF16)

    gate = gate_ref[...]
    lane = lax.broadcasted_iota(jnp.int32, gate.shape, 1)
    ge = jnp.sum(jnp.where(lane == e, gate, 0.0), -1, keepdims=True)
    xb = xb_scr[...]
    h1 = jnp.dot(xb, w1_ref[...].astype(BF16), preferred_element_type=F32)
    h3 = jnp.dot(xb, w3_ref[...].astype(BF16), preferred_element_type=F32)
    hid = (_silu(h1) * h3).astype(BF16)
    acc_scr[...] += ge * jnp.dot(hid, w2_ref[...].astype(BF16), preferred_element_type=F32)

    @pl.when((e == pl.num_programs(1) - 1) & (f == pl.num_programs(2) - 1))
    def _():
        o_ref[...] = _layer_norm(alpha * x_ref[...] + acc_scr[...], g_ref[...], b_ref[...])


def _ffn_moe(x, gate, w1, w3, w2, g, b, alpha, tm, tf):
    n, d = x.shape
    n_exp, _, dff = w1.shape
    return pl.pallas_call(
        functools.partial(_moe_kernel, alpha=alpha),
        grid=(n // tm, n_exp, dff // tf),
        in_specs=[pl.BlockSpec((tm, d), lambda i, e, f: (i, 0)),
                  pl.BlockSpec((tm, LANES), lambda i, e, f: (i, 0)),
                  pl.BlockSpec((None, d, tf), lambda i, e, f: (e, 0, f)),
                  pl.BlockSpec((None, d, tf), lambda i, e, f: (e, 0, f)),
                  pl.BlockSpec((None, tf, d), lambda i, e, f: (e, f, 0)),
                  pl.BlockSpec((1, d), lambda i, e, f: (0, 0)),
                  pl.BlockSpec((1, d), lambda i, e, f: (0, 0))],
        out_specs=pl.BlockSpec((tm, d), lambda i, e, f: (i, 0)),
        out_shape=jax.ShapeDtypeStruct((n, d), F32),
        scratch_shapes=[pltpu.VMEM((tm, d), F32), pltpu.VMEM((tm, d), BF16)],
        compiler_params=_cparams(("parallel", "arbitrary", "arbitrary")),
        name="ffn_moe",
    )(x, gate, w1, w3, w2, g.reshape(1, d), b.reshape(1, d))


def _row_tile(n, cap):
    t = cap
    while n % t:
        t //= 2
    return t


def _ff_tile(dff):
    for t in (512, 256, 128):
        if dff % t == 0:
            return t
    return dff


def kernel(x_prompt, x_sample, cache_k, cache_v, page_table, state_a, state_a_conv, state_c, ln_in_g, ln_in_b, w_in, w_out, a_conv_w, a_a_log, a_dt_bias, a_norm_w, c_lb_logits, c_norm_w, ln1_g, ln1_b, ln2_g, ln2_b, ffn_w1, ffn_w3, ffn_w2, moe_router, moe_w1, moe_w3, moe_w2):
    bp, tp, d = x_prompt.shape
    bs, ts, _ = x_sample.shape
    depth = w_in.shape[0]
    D = HEAD_DIM
    wa, wb = d // 4, d // 2
    wc = d - wa - wb
    ha, hb, hc = wa // D, wb // D, wc // D
    kc = hc * D
    alpha = (2.0 * depth) ** 0.25
    off_b = 4 * wa + 2 * ha
    off_c = off_b + 3 * wb
    p_in = off_c + 2 * kc + 2 * wc
    assert w_in.shape[2] == p_in and 2 * ha <= LANES

    seg_w = [4 * wa, wb, wb, wb, 2 * kc + 2 * wc, LANES]
    segs, lo = [], 0
    for sw in seg_w:
        segs.append((lo, lo + sw))
        lo += sw

    lb_all = jnp.cumsum(jax.nn.softmax(c_lb_logits.astype(F32), axis=0), axis=0)
    lb_all = lb_all - lb_all[0:1]

    n_pool, page = cache_k.shape[1], cache_k.shape[2]
    ck = cache_k.reshape(depth, n_pool, page * hb, D)
    cv = cache_v.reshape(depth, n_pool, page * hb, D)

    np_rows, ns_rows = bp * tp, bs * ts
    tm_p = _row_tile(np_rows, 256)
    tm_s = _row_tile(ns_rows, 256)
    xp = _ln_rows(x_prompt.reshape(np_rows, d), ln_in_g, ln_in_b, tm_p)
    xs = _ln_rows(x_sample.reshape(ns_rows, d), ln_in_g, ln_in_b, tm_s)

    outs = {k: [] for k in ("kp", "vp", "ks", "vs", "sap", "sas", "cvp", "cvs", "scp", "scs")}
    for l in range(depth):
        wl = w_in[l]
        w_perm = jnp.concatenate(
            [wl[:, :4 * wa], wl[:, off_b:], wl[:, 4 * wa:off_b],
             jnp.zeros((d, LANES - 2 * ha), wl.dtype)], axis=1).astype(BF16)
        w_o = w_out[l].astype(BF16)
        cw = a_conv_w[l]

        h_a, q_b, k_b, v_b, h_c, h_ab = _in_proj(xp, w_perm, segs, tm_p)
        h_a3 = h_a.reshape(bp, tp, 4 * wa)
        oa, sa_new = _mixa_prompt(h_a3, h_ab.reshape(bp, tp, LANES), cw, a_a_log[l], a_dt_bias[l],
                                  a_norm_w[l], ha)
        ob = _moba_prompt(q_b.reshape(bp, tp, wb), k_b.reshape(bp, tp, wb), v_b.reshape(bp, tp, wb), hb)
        oc, sc_new = _hgrn_prompt(h_c.reshape(bp, tp, 4 * wc), lb_all[l], c_norm_w[l], hc)
        outs["kp"].append(k_b.reshape(bp, tp, hb, D))
        outs["vp"].append(v_b.reshape(bp, tp, hb, D))
        outs["sap"].append(sa_new)
        conv_full = jnp.concatenate([jnp.zeros((bp, CONV_W - 1, 3 * wa), F32), h_a3[:, :, :3 * wa]], axis=1) \
            if tp < CONV_W - 1 else h_a3[:, :, :3 * wa]
        outs["cvp"].append(conv_full[:, conv_full.shape[1] - (CONV_W - 1):])
        outs["scp"].append(sc_new)
        router = moe_router[l // 2] if l % 2 else None
        x1p, gate_p = _out_proj(xp, oa.reshape(np_rows, wa), ob.reshape(np_rows, wb),
                                oc.reshape(np_rows, wc), w_o, ln1_g[l], ln1_b[l], router, alpha, tm_p)

        h_a, q_b, k_b, v_b, h_c, h_ab = _in_proj(xs, w_perm, segs, tm_s)
        h_a3 = h_a.reshape(bs, ts, 4 * wa)
        x_full = jnp.concatenate([state_a_conv[l].astype(F32), h_a3[:, :, :3 * wa]], axis=1)
        x_t = x_full.transpose(1, 2, 0)
        z_t = h_a3[:, :, 3 * wa:].transpose(1, 2, 0)
        ab_t = h_ab.reshape(bs, ts, LANES)[:, :, :max(8, 2 * ha)].transpose(1, 2, 0)
        s0_t = state_a[l].astype(F32).transpose(1, 2, 3, 0)
        oa_t, sa_t = _mixa_sample(x_t, z_t, ab_t, cw.reshape(CONV_W, 3 * wa, 1), a_a_log[l], a_dt_bias[l],
                                  a_norm_w[l], s0_t, ha, ts)
        oa = oa_t.transpose(2, 0, 1).reshape(ns_rows, wa)
        outs["sas"].append(sa_t.transpose(3, 0, 1, 2))
        outs["cvs"].append(x_full[:, x_full.shape[1] - (CONV_W - 1):])
        ob = _moba_sample(q_b.reshape(bs, ts * hb, D), k_b.reshape(bs, ts * hb, D),
                          v_b.reshape(bs, ts * hb, D), ck, cv, page_table, l, hb)
        outs["ks"].append(k_b.reshape(bs, ts, hb, D))
        outs["vs"].append(v_b.reshape(bs, ts, hb, D))
        hc_t = h_c.reshape(bs, ts, 4 * wc).transpose(1, 2, 0)
        sc0_t = state_c[l].astype(F32).transpose(1, 2, 3, 0)
        oc_t, sc_t = _hgrn_sample(hc_t, lb_all[l], c_norm_w[l], sc0_t, hc, ts)
        oc = oc_t.transpose(2, 0, 1).reshape(ns_rows, wc)
        outs["scs"].append(sc_t.transpose(3, 0, 1, 2))
        x1s, gate_s = _out_proj(xs, oa, ob.reshape(ns_rows, wb), oc, w_o, ln1_g[l], ln1_b[l], router,
                                alpha, tm_s)

        if l % 2 == 0:
            w1, w3, w2 = ffn_w1[l // 2].astype(BF16), ffn_w3[l // 2].astype(BF16), ffn_w2[l // 2].astype(BF16)
            tf = _ff_tile(w1.shape[1])
            xp = _ffn_dense(x1p, w1, w3, w2, ln2_g[l], ln2_b[l], alpha, _row_tile(np_rows, 1024), tf)
            xs = _ffn_dense(x1s, w1, w3, w2, ln2_g[l], ln2_b[l], alpha, _row_tile(ns_rows, 1024), tf)
        else:
            w1, w3, w2 = moe_w1[l // 2], moe_w3[l // 2], moe_w2[l // 2]
            tf = _ff_tile(w1.shape[2])
            xp = _ffn_moe(x1p, gate_p, w1, w3, w2, ln2_g[l], ln2_b[l], alpha, _row_tile(np_rows, 1024), tf)
            xs = _ffn_moe(x1s, gate_s, w1, w3, w2, ln2_g[l], ln2_b[l], alpha, _row_tile(ns_rows, 1024), tf)

    st = lambda key: jnp.stack(outs[key])
    return (xp.reshape(bp, tp, d), xs.reshape(bs, ts, d), st("kp"), st("vp"), st("ks"), st("vs"),
            st("sap"), st("sas"), st("cvp"), st("cvs"), st("scp"), st("scs"))
```

```python
import functools
import math

import numpy as np
import jax
import jax.numpy as jnp
from jax import lax
from jax.experimental import pallas as pl
from jax.experimental.pallas import tpu as pltpu

HEAD_DIM = 64
CONV_W = 4
CHUNK = 64
MOBA_BLOCK = 256
MOBA_TOPK = 3
MOE_TOPK = 2
LN_EPS = 1e-5
RMS_EPS = 1e-6
LANES = 128
VMEM_LIMIT = 56 * 1024 * 1024
HI = lax.Precision.HIGHEST
F32 = jnp.float32
BF16 = jnp.bfloat16


def _cparams(sem):
    return pltpu.CompilerParams(dimension_semantics=sem, vmem_limit_bytes=VMEM_LIMIT)


def _bdot(a, b):
    return jnp.dot(a.astype(BF16), b.astype(BF16), preferred_element_type=F32)


def _split2(a):
    hi = a.astype(BF16)
    return hi, (a - hi.astype(F32)).astype(BF16)


def _es(spec, a, b):
    ah, al = _split2(a)
    bh, bl = _split2(b)
    e = functools.partial(jnp.einsum, spec, preferred_element_type=F32)
    return e(ah, bh) + (e(ah, bl) + e(al, bh))


def _es_mask(spec, m, b):
    bh = b.astype(BF16)
    r = b - bh.astype(F32)
    bm = r.astype(BF16)
    bl = (r - bm.astype(F32)).astype(BF16)
    mb = m.astype(BF16)
    e = functools.partial(jnp.einsum, spec, preferred_element_type=F32)
    return e(mb, bh) + (e(mb, bm) + e(mb, bl))


def _layer_norm(x, g, b):
    mu = jnp.mean(x, -1, keepdims=True)
    xc = x - mu
    var = jnp.mean(xc * xc, -1, keepdims=True)
    return xc * lax.rsqrt(var + LN_EPS) * g + b


def _sigmoid(x):
    return 1.0 / (1.0 + jnp.exp(-x))


def _silu(x):
    return x * _sigmoid(x)


def _softplus(x):
    return jnp.maximum(x, 0.0) + jnp.log1p(jnp.exp(-jnp.abs(x)))


def _ln_kernel(x_ref, g_ref, b_ref, o_ref):
    o_ref[...] = _layer_norm(x_ref[...], g_ref[...], b_ref[...])


def _ln_rows(x, g, b, tm):
    n, d = x.shape
    return pl.pallas_call(
        _ln_kernel,
        grid=(n // tm,),
        in_specs=[pl.BlockSpec((tm, d), lambda i: (i, 0)),
                  pl.BlockSpec((1, d), lambda i: (0, 0)),
                  pl.BlockSpec((1, d), lambda i: (0, 0))],
        out_specs=pl.BlockSpec((tm, d), lambda i: (i, 0)),
        out_shape=jax.ShapeDtypeStruct((n, d), F32),
        compiler_params=_cparams(("parallel",)),
        name="ln_rows",
    )(x, g.reshape(1, d), b.reshape(1, d))


def _in_proj_kernel(x_ref, w_ref, *o_refs, segs):
    xb = x_ref[...].astype(BF16)
    for o_ref, (lo, hi) in zip(o_refs, segs):
        o_ref[...] = jnp.dot(xb, w_ref[:, lo:hi], preferred_element_type=F32)


def _in_proj(x, w, segs, tm):
    n, d = x.shape
    return pl.pallas_call(
        functools.partial(_in_proj_kernel, segs=segs),
        grid=(n // tm,),
        in_specs=[pl.BlockSpec((tm, d), lambda i: (i, 0)),
                  pl.BlockSpec(w.shape, lambda i: (0, 0))],
        out_specs=[pl.BlockSpec((tm, hi - lo), lambda i: (i, 0)) for lo, hi in segs],
        out_shape=[jax.ShapeDtypeStruct((n, hi - lo), F32) for lo, hi in segs],
        compiler_params=_cparams(("parallel",)),
        name="in_proj",
    )(x, w)


def _heads(a, nh):
    nb, c, _ = a.shape
    d = HEAD_DIM
    st = jnp.stack([a[:, :, h * d:(h + 1) * d] for h in range(nh)], axis=1)
    return st.reshape(nb * nh, c, d)


def _unheads(o, nb, nh):
    _, c, d = o.shape
    o4 = o.reshape(nb, nh, c, d)
    return jnp.concatenate([o4[:, h] for h in range(nh)], axis=-1)


def _gated_rms(o, w, z):
    o = o * lax.rsqrt(jnp.mean(o * o, -1, keepdims=True) + RMS_EPS)
    return o * w * _silu(z)


def _mixa_prompt_kernel(h_ref, ab_ref, cw_ref, al_ref, dtb_ref, nw_ref,
                        o_ref, sa_ref, s_scr, prev_scr, *, nb, nh):
    c = pl.program_id(0)
    C, D = CHUNK, HEAD_DIM
    W = nh * D
    G = nb * nh

    @pl.when(c == 0)
    def _():
        s_scr[...] = jnp.zeros_like(s_scr)
        prev_scr[...] = jnp.zeros_like(prev_scr)

    x = h_ref[:, :, 0:3 * W]
    z = h_ref[:, :, 3 * W:4 * W]
    xp = jnp.concatenate([prev_scr[...], x], axis=1)
    cw = cw_ref[...]
    y = (xp[:, 5:5 + C] * cw[0] + xp[:, 6:6 + C] * cw[1]
         + xp[:, 7:7 + C] * cw[2] + xp[:, 8:8 + C] * cw[3])
    prev_scr[...] = x[:, C - 8:C]
    y = _silu(y)

    q = _heads(y[:, :, 0:W], nh)
    k = _heads(y[:, :, W:2 * W], nh)
    v = _heads(y[:, :, 2 * W:3 * W], nh)
    zh = _heads(z, nh)
    q = q * lax.rsqrt(jnp.sum(q * q, -1, keepdims=True) + 1e-6) * (D ** -0.5)
    k = k * lax.rsqrt(jnp.sum(k * k, -1, keepdims=True) + 1e-6)

    ab = ab_ref[...]
    dt = _softplus(ab + dtb_ref[...])
    gfull = -jnp.exp(al_ref[...]) * dt
    bfull = _sigmoid(ab)

    def lane_bc(a, off):
        st = jnp.stack([jnp.broadcast_to(a[:, :, off + h:off + h + 1], (nb, C, D))
                        for h in range(nh)], axis=1)
        return st.reshape(G, C, D)

    gb = lane_bc(gfull, 0)
    beta = lane_bc(bfull, nh)

    ii = lax.broadcasted_iota(jnp.int32, (C, C), 0)
    jj = lax.broadcasted_iota(jnp.int32, (C, C), 1)

    def bcg(m):
        return jnp.broadcast_to(m.astype(F32), (G, C, C))

    gam = _es_mask('gij,gjk->gik', bcg(ii >= jj), gb)
    gam_t = jnp.einsum('gik,gjk->gij', bcg(jj == 0), gam, precision=HI,
                       preferred_element_type=F32)
    dec = jnp.exp(jnp.where(ii >= jj, gam - gam_t, -jnp.inf))

    kb = k * beta
    nmat = jnp.where(ii > jj, _es('gik,gjk->gij', kb, k) * dec, 0.0)
    xinv = (ii == jj).astype(F32) - jnp.where(ii // 2 == jj // 2, nmat, 0.0)
    s = 4
    while s <= C:
        off = jnp.where((ii // s == jj // s) & (ii // (s // 2) != jj // (s // 2)), nmat, 0.0)
        xinv = xinv - _es('gij,gjk->gik', xinv, _es('gij,gjk->gik', off, xinv))
        s *= 2

    egam = jnp.exp(gam)
    rhs = jnp.concatenate([v * beta, kb * egam], axis=-1)
    sol = _es('gij,gjk->gik', xinv, rhs)
    st = s_scr[...]
    u = sol[:, :, 0:D] - _es('gck,gkv->gcv', sol[:, :, D:2 * D], st)
    attn = _es('gik,gjk->gij', q, k) * dec
    o = _es('gck,gkv->gcv', q * egam, st) + _es('gij,gjv->giv', attn, u)
    glast = gam[:, C - 1:C, :]
    kdec = k * jnp.exp(glast - gam)
    st_new = st * jnp.exp(glast) + _es('gck,gcv->gkv', kdec, u)
    s_scr[...] = st_new

    o = _gated_rms(o, nw_ref[...], zh)
    o_ref[...] = _unheads(o, nb, nh)

    @pl.when(c == pl.num_programs(0) - 1)
    def _():
        sa_ref[...] = st_new.reshape(nb, nh, D, D)


def _mixa_prompt(h_a, h_ab, cw, a_log, dt_bias, norm_w, nh):
    nb, t, wa4 = h_a.shape
    C, D = CHUNK, HEAD_DIM
    W = nh * D
    pad = LANES - nh
    al = jnp.pad(a_log, (0, pad)).reshape(1, LANES)
    dtb = jnp.pad(dt_bias, (0, pad)).reshape(1, LANES)
    return pl.pallas_call(
        functools.partial(_mixa_prompt_kernel, nb=nb, nh=nh),
        grid=(t // C,),
        in_specs=[pl.BlockSpec((nb, C, wa4), lambda c: (0, c, 0)),
                  pl.BlockSpec((nb, C, LANES), lambda c: (0, c, 0)),
                  pl.BlockSpec((CONV_W, 3 * W), lambda c: (0, 0)),
                  pl.BlockSpec((1, LANES), lambda c: (0, 0)),
                  pl.BlockSpec((1, LANES), lambda c: (0, 0)),
                  pl.BlockSpec((1, D), lambda c: (0, 0))],
        out_specs=[pl.BlockSpec((nb, C, W), lambda c: (0, c, 0)),
                   pl.BlockSpec((nb, nh, D, D), lambda c: (0, 0, 0, 0))],
        out_shape=[jax.ShapeDtypeStruct((nb, t, W), F32),
                   jax.ShapeDtypeStruct((nb, nh, D, D), F32)],
        scratch_shapes=[pltpu.VMEM((nb * nh, D, D), F32),
                        pltpu.VMEM((nb, 8, 3 * W), F32)],
        compiler_params=_cparams(("arbitrary",)),
        name="mixa_prompt",
    )(h_a, h_ab, cw, al, dtb, norm_w.reshape(1, D))


def _hgrn_prompt_kernel(h_ref, loglb_ref, log1mlb_ref, omlb_ref, nw_ref,
                        o_ref, sc_ref, st_scr, *, nb, nh):
    c = pl.program_id(0)
    C, D = CHUNK, HEAD_DIM
    W = nh * D
    G = nb * nh

    @pl.when(c == 0)
    def _():
        st_scr[...] = jnp.zeros_like(st_scr)

    x = h_ref[...]
    qx = _heads(x[:, :, 0:W], nh)
    fx = _heads(x[:, :, W:2 * W], nh)
    iv = _heads(x[:, :, 2 * W:3 * W], nh)
    zh = _heads(x[:, :, 3 * W:4 * W], nh)

    la = loglb_ref[...]
    lbb = log1mlb_ref[...] - _softplus(-fx)
    mx = jnp.maximum(la, lbb)
    logf = mx + jnp.log1p(jnp.exp(-jnp.abs(la - lbb)))
    kc = omlb_ref[...] * _sigmoid(-fx)
    q = _silu(qx)

    ii = lax.broadcasted_iota(jnp.int32, (C, C), 0)
    jj = lax.broadcasted_iota(jnp.int32, (C, C), 1)
    ri = lax.broadcasted_iota(jnp.int32, (C, D), 0)
    gam = _es_mask('gij,gjk->gik', jnp.broadcast_to((ii >= jj).astype(F32), (G, C, C)), logf)

    def block_starts(hb):
        g4 = gam.reshape(G, C // hb, hb, D)
        starts = g4[:, :, 0:1, :]
        nxt = jnp.concatenate([starts[:, 1:], starts[:, -1:]], axis=1)
        own = jnp.broadcast_to(starts, g4.shape).reshape(G, C, D)
        nx = jnp.broadcast_to(nxt, g4.shape).reshape(G, C, D)
        return own, nx

    attn = jnp.zeros((G, C, C), F32)
    hb = C // 2
    while hb >= 8:
        s = 2 * hb
        own, nx = block_starts(hb)
        upper = (ri % s) >= hb
        qs = jnp.where(upper, q * jnp.exp(gam - own), 0.0)
        ks = jnp.where(upper, 0.0, kc * jnp.exp(jnp.where(upper, 0.0, nx - gam)))
        attn = attn + jnp.where(ii // s == jj // s, _es('gik,gjk->gij', qs, ks), 0.0)
        hb //= 2
    own, _ = block_starts(8)
    qd = q * jnp.exp(gam - own)
    kd = kc * jnp.exp(own - gam)
    attn = attn + jnp.where((ii // 8 == jj // 8) & (ii >= jj), _es('gik,gjk->gij', qd, kd), 0.0)

    st = st_scr[...]
    o = _es('gck,gvk->gcv', q * jnp.exp(gam), st) + _es('gij,gjv->giv', attn, iv)
    glast = gam[:, C - 1:C, :]
    kdec = kc * jnp.exp(glast - gam)
    st_new = st * jnp.exp(glast) + _es('gcv,gck->gvk', iv, kdec)
    st_scr[...] = st_new

    o = _gated_rms(o, nw_ref[...], zh)
    o_ref[...] = _unheads(o, nb, nh)

    @pl.when(c == pl.num_programs(0) - 1)
    def _():
        di = lax.broadcasted_iota(jnp.int32, (D, D), 0)
        dj = lax.broadcasted_iota(jnp.int32, (D, D), 1)
        eye = jnp.broadcast_to((di == dj).astype(F32), (G, D, D))
        sc_ref[...] = _es_mask('gkj,gvj->gkv', eye, st_new).reshape(nb, nh, D, D)


def _hgrn_prompt(h_c, lb, norm_w, nh):
    nb, t, wc4 = h_c.shape
    C, D = CHUNK, HEAD_DIM
    W = nh * D
    G = nb * nh

    def per_group(a):
        return jnp.tile(a.reshape(nh, 1, D), (nb, 1, 1))

    return pl.pallas_call(
        functools.partial(_hgrn_prompt_kernel, nb=nb, nh=nh),
        grid=(t // C,),
        in_specs=[pl.BlockSpec((nb, C, wc4), lambda c: (0, c, 0)),
                  pl.BlockSpec((G, 1, D), lambda c: (0, 0, 0)),
                  pl.BlockSpec((G, 1, D), lambda c: (0, 0, 0)),
                  pl.BlockSpec((G, 1, D), lambda c: (0, 0, 0)),
                  pl.BlockSpec((1, D), lambda c: (0, 0))],
        out_specs=[pl.BlockSpec((nb, C, W), lambda c: (0, c, 0)),
                   pl.BlockSpec((nb, nh, D, D), lambda c: (0, 0, 0, 0))],
        out_shape=[jax.ShapeDtypeStruct((nb, t, W), F32),
                   jax.ShapeDtypeStruct((nb, nh, D, D), F32)],
        scratch_shapes=[pltpu.VMEM((G, D, D), F32)],
        compiler_params=_cparams(("arbitrary",)),
        name="hgrn_prompt",
    )(h_c, per_group(jnp.log(lb)), per_group(jnp.log1p(-lb)), per_group(1.0 - lb),
      norm_w.reshape(1, D))


def _mixa_sample_kernel(al_ref, dtb_ref, xq_ref, xk_ref, xv_ref, z_ref, ab_ref,
                        cwq_ref, cwk_ref, cwv_ref, nw_ref, s0_ref,
                        o_ref, s_ref, kq_scr, *, nh, ts):
    h = pl.program_id(0)
    D = HEAD_DIM
    s_ref[...] = s0_ref[...]
    neg_a = -jnp.exp(al_ref[h])
    dtb = dtb_ref[h]

    def conv(x_ref, cw_ref, t):
        acc = x_ref[t] * cw_ref[0]
        for j in range(1, CONV_W):
            acc = acc + x_ref[t + j] * cw_ref[j]
        return _silu(acc)

    for t in range(ts):
        q = conv(xq_ref, cwq_ref, t)
        k = conv(xk_ref, cwk_ref, t)
        v = conv(xv_ref, cwv_ref, t)
        q = q * lax.rsqrt(jnp.sum(q * q, 0, keepdims=True) + 1e-6) * (D ** -0.5)
        k = k * lax.rsqrt(jnp.sum(k * k, 0, keepdims=True) + 1e-6)
        a = ab_ref[t, pl.ds(h, 1), :]
        b = ab_ref[t, pl.ds(nh + h, 1), :]
        decay = jnp.exp(neg_a * _softplus(a + dtb))
        beta = _sigmoid(b)
        kq_scr[0] = k
        kq_scr[1] = q

        def pass1(d, r):
            kd = kq_scr[0, pl.ds(d, 1), :]
            return r + (s_ref[0, d] * decay) * kd

        r = lax.fori_loop(0, D, pass1, jnp.zeros_like(v))
        vn = beta * (v - r)

        def pass2(d, o):
            kd = kq_scr[0, pl.ds(d, 1), :]
            qd = kq_scr[1, pl.ds(d, 1), :]
            sd = s_ref[0, d] * decay + kd * vn
            s_ref[0, d] = sd
            return o + sd * qd

        o = lax.fori_loop(0, D, pass2, jnp.zeros_like(v))
        o = o * lax.rsqrt(jnp.mean(o * o, 0, keepdims=True) + RMS_EPS)
        o_ref[t] = o * nw_ref[...] * _silu(z_ref[t])


def _mixa_sample(x_t, z_t, ab_t, cw_t, a_log, dt_bias, norm_w, s0_t, nh, ts):
    D = HEAD_DIM
    W = nh * D
    nb = x_t.shape[-1]
    tt = x_t.shape[0]
    nab = ab_t.shape[1]
    smem = pl.BlockSpec(memory_space=pltpu.SMEM)
    return pl.pallas_call(
        functools.partial(_mixa_sample_kernel, nh=nh, ts=ts),
        grid=(nh,),
        in_specs=[smem, smem,
                  pl.BlockSpec((tt, D, nb), lambda h: (0, h, 0)),
                  pl.BlockSpec((tt, D, nb), lambda h: (0, nh + h, 0)),
                  pl.BlockSpec((tt, D, nb), lambda h: (0, 2 * nh + h, 0)),
                  pl.BlockSpec((ts, D, nb), lambda h: (0, h, 0)),
                  pl.BlockSpec((ts, nab, nb), lambda h: (0, 0, 0)),
                  pl.BlockSpec((CONV_W, D, 1), lambda h: (0, h, 0)),
                  pl.BlockSpec((CONV_W, D, 1), lambda h: (0, nh + h, 0)),
                  pl.BlockSpec((CONV_W, D, 1), lambda h: (0, 2 * nh + h, 0)),
                  pl.BlockSpec((D, 1), lambda h: (0, 0)),
                  pl.BlockSpec((1, D, D, nb), lambda h: (h, 0, 0, 0))],
        out_specs=[pl.BlockSpec((ts, D, nb), lambda h: (0, h, 0)),
                   pl.BlockSpec((1, D, D, nb), lambda h: (h, 0, 0, 0))],
        out_shape=[jax.ShapeDtypeStruct((ts, W, nb), F32),
                   jax.ShapeDtypeStruct((nh, D, D, nb), F32)],
        scratch_shapes=[pltpu.VMEM((2, D, nb), F32)],
        compiler_params=_cparams(("arbitrary",)),
        name="mixa_sample",
    )(a_log, dt_bias, x_t, x_t, x_t, z_t, ab_t, cw_t, cw_t, cw_t,
      norm_w.reshape(D, 1), s0_t)


def _hgrn_sample_kernel(q_ref, f_ref, i_ref, z_ref, lb_ref, nw_ref, s0_ref,
                        o_ref, s_ref, kq_scr, *, ts):
    D = HEAD_DIM
    s_ref[...] = s0_ref[...]
    lb = lb_ref[...]
    for t in range(ts):
        fx = f_ref[t]
        f = lb + (1.0 - lb) * _sigmoid(fx)
        kq_scr[0] = f
        kq_scr[1] = (1.0 - lb) * _sigmoid(-fx)
        kq_scr[2] = _silu(q_ref[t])
        v = i_ref[t]

        def body(d, o):
            fd = kq_scr[0, pl.ds(d, 1), :]
            kd = kq_scr[1, pl.ds(d, 1), :]
            qd = kq_scr[2, pl.ds(d, 1), :]
            sd = s_ref[0, d] * fd + kd * v
            s_ref[0, d] = sd
            return o + sd * qd

        o = lax.fori_loop(0, D, body, jnp.zeros_like(v))
        o = o * lax.rsqrt(jnp.mean(o * o, 0, keepdims=True) + RMS_EPS)
        o_ref[t] = o * nw_ref[...] * _silu(z_ref[t])


def _hgrn_sample(hc_t, lb, norm_w, s0_t, nh, ts):
    D = HEAD_DIM
    W = nh * D
    nb = hc_t.shape[-1]
    return pl.pallas_call(
        functools.partial(_hgrn_sample_kernel, ts=ts),
        grid=(nh,),
        in_specs=[pl.BlockSpec((ts, D, nb), lambda h: (0, h, 0)),
                  pl.BlockSpec((ts, D, nb), lambda h: (0, nh + h, 0)),
                  pl.BlockSpec((ts, D, nb), lambda h: (0, 2 * nh + h, 0)),
                  pl.BlockSpec((ts, D, nb), lambda h: (0, 3 * nh + h, 0)),
                  pl.BlockSpec((D, 1), lambda h: (h, 0)),
                  pl.BlockSpec((D, 1), lambda h: (0, 0)),
                  pl.BlockSpec((1, D, D, nb), lambda h: (h, 0, 0, 0))],
        out_specs=[pl.BlockSpec((ts, D, nb), lambda h: (0, h, 0)),
                   pl.BlockSpec((1, D, D, nb), lambda h: (h, 0, 0, 0))],
        out_shape=[jax.ShapeDtypeStruct((ts, W, nb), F32),
                   jax.ShapeDtypeStruct((nh, D, D, nb), F32)],
        scratch_shapes=[pltpu.VMEM((3, D, nb), F32)],
        compiler_params=_cparams(("arbitrary",)),
        name="hgrn_sample",
    )(hc_t, hc_t, hc_t, hc_t, lb.reshape(W, 1), norm_w.reshape(D, 1), s0_t)


def _block_mean_kernel(k_ref, o_ref):
    o_ref[...] = jnp.mean(k_ref[...], axis=0, keepdims=True)


def _block_mean(k):
    nb, t, w = k.shape
    nblk = t // MOBA_BLOCK
    out = pl.pallas_call(
        _block_mean_kernel,
        grid=(nb, nblk),
        in_specs=[pl.BlockSpec((None, MOBA_BLOCK, w), lambda b, n: (b, n, 0))],
        out_specs=pl.BlockSpec((None, None, 1, w), lambda b, n: (b, n, 0, 0)),
        out_shape=jax.ShapeDtypeStruct((nb, nblk, 1, w), F32),
        compiler_params=_cparams(("parallel", "parallel")),
        name="moba_block_mean",
    )(k)
    return out.reshape(nb, nblk, w)


def _alibi_slope(h, nh):
    return float(2.0 ** (-8.0 * (h + 1) / nh))


LOG2E = 1.4426950408889634


def _moba_prompt_kernel(qi_ref, kj_ref, q_ref, k_ref, v_ref, km_ref, o_ref,
                        acc_scr, m_scr, l_scr, sel_scr, qs_scr, bias_scr, *, nh, nblk):
    step = pl.program_id(1)
    qi = qi_ref[step]
    kj = kj_ref[step]
    BS, D = MOBA_BLOCK, HEAD_DIM
    kpos = lax.broadcasted_iota(jnp.int32, (BS, BS), 0)
    qpos = lax.broadcasted_iota(jnp.int32, (BS, BS), 1)

    @pl.when(step == 0)
    def _():
        rel0 = (qpos - kpos).astype(F32)
        for h in range(nh):
            bias_scr[h] = (_alibi_slope(h, nh) * LOG2E) * rel0

    @pl.when(kj == 0)
    def _():
        m_scr[...] = jnp.full_like(m_scr, -jnp.inf)
        l_scr[...] = jnp.zeros_like(l_scr)
        acc_scr[...] = jnp.zeros_like(acc_scr)
        nn = lax.broadcasted_iota(jnp.int32, (nblk, BS), 0)
        elig = nn < qi
        for h in range(nh):
            qh = q_ref[:, h * D:(h + 1) * D]
            qs_scr[h] = (qh * (D ** -0.5 * LOG2E)).astype(BF16)
            gate = _es('nd,qd->nq', km_ref[:, h * D:(h + 1) * D], qh)
            gate = jnp.where(elig, gate, -jnp.inf)
            rank = jnp.zeros((nblk, BS), F32)
            for m in range(nblk):
                gm = gate[m:m + 1, :]
                ahead = (gm > gate) | ((gm == gate) & (nn > m))
                rank = rank + jnp.where(ahead, 1.0, 0.0)
            sel_scr[h] = jnp.where(elig & (rank < MOBA_TOPK), 0.0, -jnp.inf)

    def attend(mask_fn):
        hs = range(nh)
        raw = [jnp.einsum('kd,qd->kq', k_ref[:, h * D:(h + 1) * D].astype(BF16), qs_scr[h],
                          preferred_element_type=F32) for h in hs]
        st = [mask_fn(h, raw[h] - bias_scr[h]) for h in hs]
        m_old = m_scr[...]
        l_old = l_scr[...]
        m_new = jnp.concatenate([jnp.max(st[h], axis=0, keepdims=True) for h in hs], axis=0)
        m_new = jnp.maximum(m_old, m_new)
        m_safe = jnp.where(m_new == -jnp.inf, 0.0, m_new)
        alpha = jnp.exp2(m_old - m_safe)
        p = [jnp.exp2(st[h] - m_safe[h:h + 1, :]) for h in hs]
        l_scr[...] = alpha * l_old + jnp.concatenate(
            [jnp.sum(p[h], axis=0, keepdims=True) for h in hs], axis=0)
        m_scr[...] = m_new
        pv = [jnp.einsum('kd,kq->dq', v_ref[:, h * D:(h + 1) * D].astype(BF16), p[h].astype(BF16),
                         preferred_element_type=F32) for h in hs]
        for h in hs:
            acc_scr[h] = alpha[h:h + 1, :] * acc_scr[h] + pv[h]

    @pl.when(kj < qi)
    def _():
        off = ((qi - kj) * BS).astype(F32)
        attend(lambda h, s: s + (sel_scr[h, pl.ds(kj, 1), :] - (_alibi_slope(h, nh) * LOG2E) * off))

    @pl.when(kj == qi)
    def _():
        attend(lambda h, s: jnp.where(qpos >= kpos, s, -jnp.inf))
        outs = [acc_scr[h] / l_scr[h:h + 1, :] for h in range(nh)]
        o_ref[...] = jnp.concatenate(outs, axis=0).T


def _moba_prompt(q, k, v, nh):
    nb, t, w = q.shape
    BS, D = MOBA_BLOCK, HEAD_DIM
    nblk = t // BS
    km = _block_mean(k)
    pairs = [(i, j) for i in range(nblk) for j in range(i + 1)]
    qi = jnp.asarray([p[0] for p in pairs], jnp.int32)
    kj = jnp.asarray([p[1] for p in pairs], jnp.int32)
    grid_spec = pltpu.PrefetchScalarGridSpec(
        num_scalar_prefetch=2,
        grid=(nb, len(pairs)),
        in_specs=[pl.BlockSpec((None, BS, w), lambda b, s, qi, kj: (b, qi[s], 0)),
                  pl.BlockSpec((None, BS, w), lambda b, s, qi, kj: (b, kj[s], 0)),
                  pl.BlockSpec((None, BS, w), lambda b, s, qi, kj: (b, kj[s], 0)),
                  pl.BlockSpec((None, nblk, w), lambda b, s, qi, kj: (b, 0, 0))],
        out_specs=pl.BlockSpec((None, BS, w), lambda b, s, qi, kj: (b, qi[s], 0)),
        scratch_shapes=[pltpu.VMEM((nh, D, BS), F32),
                        pltpu.VMEM((nh, BS), F32),
                        pltpu.VMEM((nh, BS), F32),
                        pltpu.VMEM((nh, nblk, BS), F32),
                        pltpu.VMEM((nh, BS, D), BF16),
                        pltpu.VMEM((nh, BS, BS), F32)])
    return pl.pallas_call(
        functools.partial(_moba_prompt_kernel, nh=nh, nblk=nblk),
        grid_spec=grid_spec,
        out_shape=jax.ShapeDtypeStruct((nb, t, w), F32),
        compiler_params=_cparams(("parallel", "arbitrary")),
        name="moba_prompt",
    )(qi, kj, q, k, v, km)


def _moba_sample_kernel(pt_ref, q_ref, kn_ref, vn_ref, *refs, nh, ts, past_len, bps):
    npg = 2 * bps
    k_refs, v_refs = refs[:npg], refs[npg:2 * npg]
    o_ref, m_scr, l_scr, g_scr, acc_scr = refs[2 * npg:]
    step = pl.program_id(1)
    BS, D = MOBA_BLOCK, HEAD_DIM
    page = BS // 2
    TP = q_ref.shape[1]
    scale = D ** -0.5
    q = q_ref[...]
    qb = q.astype(BF16)

    tpos = lax.broadcasted_iota(jnp.int32, (nh, TP, 1), 1).astype(F32)
    hidx = lax.broadcasted_iota(jnp.int32, (nh, TP, 1), 0)
    slope = jnp.zeros((nh, TP, 1), F32)
    for h in range(nh):
        slope = jnp.where(hidx == h, _alibi_slope(h, nh), slope)
    kofs = lax.broadcasted_iota(jnp.int32, (nh, TP, page), 2).astype(F32)
    qpos = past_len + tpos

    pages = range(npg)
    raw = [jnp.einsum('htd,hdk->htk', qb, k_refs[i][...].astype(BF16), preferred_element_type=F32)
           for i in pages]
    scores = []
    for i in pages:
        kpos = ((step * bps + i // 2) * BS + (i % 2) * page).astype(F32) + kofs
        scores.append(raw[i] * scale - slope * (qpos - kpos))
    m_blk = [jnp.maximum(jnp.max(scores[2 * b], axis=-1, keepdims=True),
                         jnp.max(scores[2 * b + 1], axis=-1, keepdims=True)) for b in range(bps)]
    p = [jnp.exp(scores[i] - m_blk[i // 2]) for i in pages]
    pv = [jnp.einsum('htk,hdk->htd', p[i].astype(BF16), v_refs[i][...].astype(BF16),
                     preferred_element_type=F32) for i in pages]
    for b in range(bps):
        n = step * bps + b
        l_n = (jnp.sum(p[2 * b], axis=-1, keepdims=True)
               + jnp.sum(p[2 * b + 1], axis=-1, keepdims=True))
        gate = (jnp.sum(raw[2 * b], axis=-1, keepdims=True)
                + jnp.sum(raw[2 * b + 1], axis=-1, keepdims=True)) * (1.0 / BS)
        m_scr[n] = jnp.broadcast_to(m_blk[b], (nh, TP, LANES))
        l_scr[n] = jnp.broadcast_to(l_n, (nh, TP, LANES))
        g_scr[n] = jnp.broadcast_to(gate, (nh, TP, LANES))
        acc_scr[n] = pv[2 * b] + pv[2 * b + 1]

    @pl.when(step == pl.num_programs(1) - 1)
    def _():
        nb_past = past_len // BS
        gates = [g_scr[i][:, :, 0:1] for i in range(nb_past)]
        sels = []
        for i in range(nb_past):
            rank = jnp.zeros((nh, TP, 1), F32)
            for m in range(nb_past):
                if m == i:
                    continue
                ahead = (gates[m] > gates[i]) if m > i else (gates[m] >= gates[i])
                rank = rank + jnp.where(ahead, 1.0, 0.0)
            sels.append(rank < MOBA_TOPK)
        kn = kn_ref[...]
        vn = vn_ref[...]
        s_own = []
        for j in range(ts):
            sj = jnp.sum(q * kn[:, j:j + 1, :], axis=-1, keepdims=True) * scale
            sj = sj - slope * (tpos - j)
            s_own.append(jnp.where(tpos >= j, sj, -jnp.inf))
        m_tot = s_own[0]
        for j in range(1, ts):
            m_tot = jnp.maximum(m_tot, s_own[j])
        for i in range(nb_past):
            m_tot = jnp.maximum(m_tot, jnp.where(sels[i], m_scr[i][:, :, 0:1], -jnp.inf))
        num = jnp.zeros((nh, TP, D), F32)
        den = jnp.zeros((nh, TP, 1), F32)
        for j in range(ts):
            pj = jnp.exp(s_own[j] - m_tot)
            num = num + pj * vn[:, j:j + 1, :]
            den = den + pj
        for i in range(nb_past):
            wi = jnp.where(sels[i], jnp.exp(m_scr[i][:, :, 0:1] - m_tot), 0.0)
            num = num + wi * acc_scr[i]
            den = den + wi * l_scr[i][:, :, 0:1]
        o_ref[...] = num / den


def _moba_sample(q, kn, vn, cache_kt, cache_vt, page_table, layer, ts):
    nb, nh, TP, D = q.shape
    page = cache_kt.shape[4]
    n_pages = page_table.shape[1]
    past_len = n_pages * page
    BS = MOBA_BLOCK
    assert BS == 2 * page and past_len % BS == 0 and ts <= TP
    nblk = past_len // BS
    bps = 2 if nblk % 2 == 0 else 1
    npg = 2 * bps
    pt = page_table.reshape(-1)

    def page_spec(i):
        return pl.BlockSpec((None, None, nh, D, page),
                            lambda b, s, pt: (layer, pt[b * n_pages + s * npg + i], 0, 0, 0))

    tok = pl.BlockSpec((None, nh, TP, D), lambda b, s, pt: (b, 0, 0, 0))
    pages = [page_spec(i) for i in range(npg)]
    grid_spec = pltpu.PrefetchScalarGridSpec(
        num_scalar_prefetch=1,
        grid=(nb, nblk // bps),
        in_specs=[tok, tok, tok] + pages + pages,
        out_specs=tok,
        scratch_shapes=[pltpu.VMEM((nblk, nh, TP, LANES), F32),
                        pltpu.VMEM((nblk, nh, TP, LANES), F32),
                        pltpu.VMEM((nblk, nh, TP, LANES), F32),
                        pltpu.VMEM((nblk, nh, TP, D), F32)])
    return pl.pallas_call(
        functools.partial(_moba_sample_kernel, nh=nh, ts=ts, past_len=past_len, bps=bps),
        grid_spec=grid_spec,
        out_shape=jax.ShapeDtypeStruct((nb, nh, TP, D), F32),
        compiler_params=_cparams(("parallel", "arbitrary")),
        name="moba_sample",
    )(pt, q, kn, vn, *([cache_kt] * npg), *([cache_vt] * npg))


def _router_gates(y, r, n_exp):
    logits = _es('mk,kn->mn', y, r)
    lane = lax.broadcasted_iota(jnp.int32, logits.shape, 1)
    logits = jnp.where(lane < n_exp, logits, -jnp.inf)
    m1 = jnp.max(logits, -1, keepdims=True)
    i1 = jnp.min(jnp.where(logits == m1, lane, LANES), -1, keepdims=True)
    first = lane == i1
    rest = jnp.where(first, -jnp.inf, logits)
    m2 = jnp.max(rest, -1, keepdims=True)
    i2 = jnp.min(jnp.where(rest == m2, lane, LANES), -1, keepdims=True)
    second = lane == i2
    e = jnp.exp(m2 - m1)
    den = 1.0 + e
    return jnp.where(first, 1.0 / den, 0.0) + jnp.where(second, e / den, 0.0)


def _out_proj_kernel(*refs, alpha, wa, wb, n_exp):
    if n_exp:
        x_ref, oa_ref, ob_ref, oc_ref, w_ref, g_ref, b_ref, r_ref, o_ref, gate_ref = refs
    else:
        x_ref, oa_ref, ob_ref, oc_ref, w_ref, g_ref, b_ref, o_ref = refs
    acc = jnp.dot(oa_ref[...].astype(BF16), w_ref[0:wa, :], preferred_element_type=F32)
    acc = acc + jnp.dot(ob_ref[...].astype(BF16), w_ref[wa:wa + wb, :], preferred_element_type=F32)
    acc = acc + jnp.dot(oc_ref[...].astype(BF16), w_ref[wa + wb:, :], preferred_element_type=F32)
    y = _layer_norm(alpha * x_ref[...] + acc, g_ref[...], b_ref[...])
    o_ref[...] = y
    if n_exp:
        gate_ref[...] = _router_gates(y, r_ref[...], n_exp)


def _out_proj(x, oa, ob, oc, w, g, b, router, alpha, tm):
    n, d = x.shape
    wa, wb, wc = oa.shape[1], ob.shape[1], oc.shape[1]
    n_exp = 0 if router is None else router.shape[1]
    row = lambda width: pl.BlockSpec((tm, width), lambda i: (i, 0))
    full = lambda shape: pl.BlockSpec(shape, lambda i: (0, 0))
    in_specs = [row(d), row(wa), row(wb), row(wc), full(w.shape), full((1, d)), full((1, d))]
    args = [x, oa, ob, oc, w, g.reshape(1, d), b.reshape(1, d)]
    out_specs = [row(d)]
    out_shape = [jax.ShapeDtypeStruct((n, d), F32)]
    if n_exp:
        in_specs.append(full((d, LANES)))
        args.append(jnp.pad(router, ((0, 0), (0, LANES - n_exp))))
        out_specs.append(row(LANES))
        out_shape.append(jax.ShapeDtypeStruct((n, LANES), F32))
    res = pl.pallas_call(
        functools.partial(_out_proj_kernel, alpha=alpha, wa=wa, wb=wb, n_exp=n_exp),
        grid=(n // tm,),
        in_specs=in_specs, out_specs=out_specs, out_shape=out_shape,
        compiler_params=_cparams(("parallel",)),
        name="out_proj_ln",
    )(*args)
    return (res[0], res[1]) if n_exp else (res[0], None)


def _ffn_kernel(x_ref, w1_ref, w3_ref, w2_ref, g_ref, b_ref, o_ref, acc_scr, xb_scr, *, alpha):
    f = pl.program_id(1)

    @pl.when(f == 0)
    def _():
        acc_scr[...] = jnp.zeros_like(acc_scr)
        xb_scr[...] = x_ref[...].astype(BF16)

    xb = xb_scr[...]
    h1 = jnp.dot(xb, w1_ref[...].astype(BF16), preferred_element_type=F32)
    h3 = jnp.dot(xb, w3_ref[...].astype(BF16), preferred_element_type=F32)
    hid = (_silu(h1) * h3).astype(BF16)
    acc_scr[...] += jnp.dot(hid, w2_ref[...].astype(BF16), preferred_element_type=F32)

    @pl.when(f == pl.num_programs(1) - 1)
    def _():
        o_ref[...] = _layer_norm(alpha * x_ref[...] + acc_scr[...], g_ref[...], b_ref[...])


def _ffn_dense(x, w1, w3, w2, g, b, alpha, tm, tf):
    n, d = x.shape
    dff = w1.shape[1]
    return pl.pallas_call(
        functools.partial(_ffn_kernel, alpha=alpha),
        grid=(n // tm, dff // tf),
        in_specs=[pl.BlockSpec((tm, d), lambda i, f: (i, 0)),
                  pl.BlockSpec((d, tf), lambda i, f: (0, f)),
                  pl.BlockSpec((d, tf), lambda i, f: (0, f)),
                  pl.BlockSpec((tf, d), lambda i, f: (f, 0)),
                  pl.BlockSpec((1, d), lambda i, f: (0, 0)),
                  pl.BlockSpec((1, d), lambda i, f: (0, 0))],
        out_specs=pl.BlockSpec((tm, d), lambda i, f: (i, 0)),
        out_shape=jax.ShapeDtypeStruct((n, d), F32),
        scratch_shapes=[pltpu.VMEM((tm, d), F32), pltpu.VMEM((tm, d), BF16)],
        compiler_params=_cparams(("parallel", "arbitrary")),
        name="ffn_dense",
    )(x, w1, w3, w2, g.reshape(1, d), b.reshape(1, d))


def _moe_kernel(x_ref, gate_ref, w1_ref, w3_ref, w2_ref, g_ref, b_ref, o_ref,
                acc_scr, xb_scr, *, alpha):
    e = pl.program_id(1)
    f = pl.program_id(2)

    @pl.when((e == 0) & (f == 0))
    def _():
        acc_scr[...] = jnp.zeros_like(acc_scr)
        xb_scr[...] = x_ref[...].astype(BF16)

    gate = gate_ref[...]
    lane = lax.broadcasted_iota(jnp.int32, gate.shape, 1)
    ge = jnp.sum(jnp.where(lane == e, gate, 0.0), -1, keepdims=True)
    xb = xb_scr[...]
    h1 = jnp.dot(xb, w1_ref[...].astype(BF16), preferred_element_type=F32)
    h3 = jnp.dot(xb, w3_ref[...].astype(BF16), preferred_element_type=F32)
    hid = (_silu(h1) * h3).astype(BF16)
    acc_scr[...] += ge * jnp.dot(hid, w2_ref[...].astype(BF16), preferred_element_type=F32)

    @pl.when((e == pl.num_programs(1) - 1) & (f == pl.num_programs(2) - 1))
    def _():
        o_ref[...] = _layer_norm(alpha * x_ref[...] + acc_scr[...], g_ref[...], b_ref[...])


def _ffn_moe(x, gate, w1, w3, w2, li, g, b, alpha, tm, tf):
    n, d = x.shape
    _, n_exp, _, dff = w1.shape
    return pl.pallas_call(
        functools.partial(_moe_kernel, alpha=alpha),
        grid=(n // tm, n_exp, dff // tf),
        in_specs=[pl.BlockSpec((tm, d), lambda i, e, f: (i, 0)),
                  pl.BlockSpec((tm, LANES), lambda i, e, f: (i, 0)),
                  pl.BlockSpec((None, None, d, tf), lambda i, e, f: (li, e, 0, f)),
                  pl.BlockSpec((None, None, d, tf), lambda i, e, f: (li, e, 0, f)),
                  pl.BlockSpec((None, None, tf, d), lambda i, e, f: (li, e, f, 0)),
                  pl.BlockSpec((1, d), lambda i, e, f: (0, 0)),
                  pl.BlockSpec((1, d), lambda i, e, f: (0, 0))],
        out_specs=pl.BlockSpec((tm, d), lambda i, e, f: (i, 0)),
        out_shape=jax.ShapeDtypeStruct((n, d), F32),
        scratch_shapes=[pltpu.VMEM((tm, d), F32), pltpu.VMEM((tm, d), BF16)],
        compiler_params=_cparams(("parallel", "arbitrary", "arbitrary")),
        name="ffn_moe",
    )(x, gate, w1, w3, w2, g.reshape(1, d), b.reshape(1, d))


def _row_tile(n, cap):
    t = cap
    while n % t:
        t //= 2
    return t


def _ff_tile(dff):
    for t in (512, 256, 128):
        if dff % t == 0:
            return t
    return dff


def kernel(x_prompt, x_sample, cache_k, cache_v, page_table, state_a, state_a_conv, state_c, ln_in_g, ln_in_b, w_in, w_out, a_conv_w, a_a_log, a_dt_bias, a_norm_w, c_lb_logits, c_norm_w, ln1_g, ln1_b, ln2_g, ln2_b, ffn_w1, ffn_w3, ffn_w2, moe_router, moe_w1, moe_w3, moe_w2):
    bp, tp, d = x_prompt.shape
    bs, ts, _ = x_sample.shape
    depth = w_in.shape[0]
    D = HEAD_DIM
    wa, wb = d // 4, d // 2
    wc = d - wa - wb
    ha, hb, hc = wa // D, wb // D, wc // D
    kc = hc * D
    alpha = (2.0 * depth) ** 0.25
    off_b = 4 * wa + 2 * ha
    off_c = off_b + 3 * wb
    p_in = off_c + 2 * kc + 2 * wc
    assert w_in.shape[2] == p_in and 2 * ha <= LANES

    seg_w = [4 * wa, wb, wb, wb, 2 * kc + 2 * wc, LANES]
    segs, lo = [], 0
    for sw in seg_w:
        segs.append((lo, lo + sw))
        lo += sw

    lb_all = jnp.cumsum(jax.nn.softmax(c_lb_logits.astype(F32), axis=0), axis=0)
    lb_all = lb_all - lb_all[0:1]

    n_pool, page = cache_k.shape[1], cache_k.shape[2]
    ck = cache_k.transpose(0, 1, 3, 4, 2)
    cv = cache_v.transpose(0, 1, 3, 4, 2)
    tp_pad = -(-ts // 8) * 8

    np_rows, ns_rows = bp * tp, bs * ts
    tm_p = _row_tile(np_rows, 256)
    tm_s = _row_tile(ns_rows, 256)
    xp = _ln_rows(x_prompt.reshape(np_rows, d), ln_in_g, ln_in_b, tm_p)
    xs = _ln_rows(x_sample.reshape(ns_rows, d), ln_in_g, ln_in_b, tm_s)

    outs = {k: [] for k in ("kp", "vp", "ks", "vs", "sap", "sas", "cvp", "cvs", "scp", "scs")}
    for l in range(depth):
        wl = w_in[l]
        w_perm = jnp.concatenate(
            [wl[:, :4 * wa], wl[:, off_b:], wl[:, 4 * wa:off_b],
             jnp.zeros((d, LANES - 2 * ha), wl.dtype)], axis=1).astype(BF16)
        w_o = w_out[l].astype(BF16)
        cw = a_conv_w[l]

        h_a, q_b, k_b, v_b, h_c, h_ab = _in_proj(xp, w_perm, segs, tm_p)
        h_a3 = h_a.reshape(bp, tp, 4 * wa)
        oa, sa_new = _mixa_prompt(h_a3, h_ab.reshape(bp, tp, LANES), cw, a_a_log[l], a_dt_bias[l],
                                  a_norm_w[l], ha)
        ob = _moba_prompt(q_b.reshape(bp, tp, wb), k_b.reshape(bp, tp, wb), v_b.reshape(bp, tp, wb), hb)
        oc, sc_new = _hgrn_prompt(h_c.reshape(bp, tp, 4 * wc), lb_all[l], c_norm_w[l], hc)
        outs["kp"].append(k_b.reshape(bp, tp, hb, D))
        outs["vp"].append(v_b.reshape(bp, tp, hb, D))
        outs["sap"].append(sa_new)
        conv_full = jnp.concatenate([jnp.zeros((bp, CONV_W - 1, 3 * wa), F32), h_a3[:, :, :3 * wa]], axis=1) \
            if tp < CONV_W - 1 else h_a3[:, :, :3 * wa]
        outs["cvp"].append(conv_full[:, conv_full.shape[1] - (CONV_W - 1):])
        outs["scp"].append(sc_new)
        router = moe_router[l // 2] if l % 2 else None
        x1p, gate_p = _out_proj(xp, oa.reshape(np_rows, wa), ob.reshape(np_rows, wb),
                                oc.reshape(np_rows, wc), w_o, ln1_g[l], ln1_b[l], router, alpha, tm_p)

        h_a, q_b, k_b, v_b, h_c, h_ab = _in_proj(xs, w_perm, segs, tm_s)
        h_a3 = h_a.reshape(bs, ts, 4 * wa)
        x_full = jnp.concatenate([state_a_conv[l].astype(F32), h_a3[:, :, :3 * wa]], axis=1)
        x_t = x_full.transpose(1, 2, 0)
        z_t = h_a3[:, :, 3 * wa:].transpose(1, 2, 0)
        ab_t = h_ab.reshape(bs, ts, LANES)[:, :, :max(8, 2 * ha)].transpose(1, 2, 0)
        s0_t = state_a[l].astype(F32).transpose(1, 2, 3, 0)
        oa_t, sa_t = _mixa_sample(x_t, z_t, ab_t, cw.reshape(CONV_W, 3 * wa, 1), a_a_log[l], a_dt_bias[l],
                                  a_norm_w[l], s0_t, ha, ts)
        oa = oa_t.transpose(2, 0, 1).reshape(ns_rows, wa)
        outs["sas"].append(sa_t.transpose(3, 0, 1, 2))
        outs["cvs"].append(x_full[:, x_full.shape[1] - (CONV_W - 1):])
        def head_major(a):
            a = a.reshape(bs, ts, hb, D).transpose(0, 2, 1, 3)
            return jnp.pad(a, ((0, 0), (0, 0), (0, tp_pad - ts), (0, 0)))

        ob = _moba_sample(head_major(q_b), head_major(k_b), head_major(v_b), ck, cv, page_table, l, ts)
        ob = ob[:, :, :ts].transpose(0, 2, 1, 3)
        outs["ks"].append(k_b.reshape(bs, ts, hb, D))
        outs["vs"].append(v_b.reshape(bs, ts, hb, D))
        hc_t = h_c.reshape(bs, ts, 4 * wc).transpose(1, 2, 0)
        sc0_t = state_c[l].astype(F32).transpose(1, 2, 3, 0)
        oc_t, sc_t = _hgrn_sample(hc_t, lb_all[l], c_norm_w[l], sc0_t, hc, ts)
        oc = oc_t.transpose(2, 0, 1).reshape(ns_rows, wc)
        outs["scs"].append(sc_t.transpose(3, 0, 1, 2))
        x1s, gate_s = _out_proj(xs, oa, ob.reshape(ns_rows, wb), oc, w_o, ln1_g[l], ln1_b[l], router,
                                alpha, tm_s)

        if l % 2 == 0:
            w1, w3, w2 = ffn_w1[l // 2].astype(BF16), ffn_w3[l // 2].astype(BF16), ffn_w2[l // 2].astype(BF16)
            tf = _ff_tile(w1.shape[1])
            xp = _ffn_dense(x1p, w1, w3, w2, ln2_g[l], ln2_b[l], alpha, _row_tile(np_rows, 1024), tf)
            xs = _ffn_dense(x1s, w1, w3, w2, ln2_g[l], ln2_b[l], alpha, _row_tile(ns_rows, 1024), tf)
        else:
            tf = _ff_tile(moe_w1.shape[3])
            xp = _ffn_moe(x1p, gate_p, moe_w1, moe_w3, moe_w2, l // 2, ln2_g[l], ln2_b[l], alpha,
                          _row_tile(np_rows, 1024), tf)
            xs = _ffn_moe(x1s, gate_s, moe_w1, moe_w3, moe_w2, l // 2, ln2_g[l], ln2_b[l], alpha,
                          _row_tile(ns_rows, 1024), tf)

    st = lambda key: jnp.stack(outs[key])
    return (xp.reshape(bp, tp, d), xs.reshape(bs, ts, d), st("kp"), st("vp"), st("ks"), st("vs"),
            st("sap"), st("sas"), st("cvp"), st("cvs"), st("scp"), st("scs"))
```

```python
import functools
import math

import numpy as np
import jax
import jax.numpy as jnp
from jax import lax
from jax.experimental import pallas as pl
from jax.experimental.pallas import tpu as pltpu

HEAD_DIM = 64
CONV_W = 4
CHUNK = 64
MOBA_BLOCK = 256
MOBA_TOPK = 3
MOE_TOPK = 2
LN_EPS = 1e-5
RMS_EPS = 1e-6
LANES = 128
VMEM_LIMIT = 56 * 1024 * 1024
HI = lax.Precision.HIGHEST
F32 = jnp.float32
BF16 = jnp.bfloat16


def _cparams(sem):
    return pltpu.CompilerParams(dimension_semantics=sem, vmem_limit_bytes=VMEM_LIMIT)


def _bdot(a, b):
    return jnp.dot(a.astype(BF16), b.astype(BF16), preferred_element_type=F32)


def _split2(a):
    hi = a.astype(BF16)
    return hi, (a - hi.astype(F32)).astype(BF16)


def _es(spec, a, b):
    ah, al = _split2(a)
    bh, bl = _split2(b)
    e = functools.partial(jnp.einsum, spec, preferred_element_type=F32)
    return e(ah, bh) + (e(ah, bl) + e(al, bh))


def _es_mask(spec, m, b):
    bh = b.astype(BF16)
    r = b - bh.astype(F32)
    bm = r.astype(BF16)
    bl = (r - bm.astype(F32)).astype(BF16)
    mb = m.astype(BF16)
    e = functools.partial(jnp.einsum, spec, preferred_element_type=F32)
    return e(mb, bh) + (e(mb, bm) + e(mb, bl))


def _layer_norm(x, g, b):
    mu = jnp.mean(x, -1, keepdims=True)
    xc = x - mu
    var = jnp.mean(xc * xc, -1, keepdims=True)
    return xc * lax.rsqrt(var + LN_EPS) * g + b


def _sigmoid(x):
    return 1.0 / (1.0 + jnp.exp(-x))


def _silu(x):
    return x * _sigmoid(x)


def _softplus(x):
    return jnp.maximum(x, 0.0) + jnp.log1p(jnp.exp(-jnp.abs(x)))


def _ln_kernel(x_ref, g_ref, b_ref, o_ref):
    o_ref[...] = _layer_norm(x_ref[...], g_ref[...], b_ref[...])


def _ln_rows(x, g, b, tm):
    n, d = x.shape
    return pl.pallas_call(
        _ln_kernel,
        grid=(n // tm,),
        in_specs=[pl.BlockSpec((tm, d), lambda i: (i, 0)),
                  pl.BlockSpec((1, d), lambda i: (0, 0)),
                  pl.BlockSpec((1, d), lambda i: (0, 0))],
        out_specs=pl.BlockSpec((tm, d), lambda i: (i, 0)),
        out_shape=jax.ShapeDtypeStruct((n, d), F32),
        compiler_params=_cparams(("parallel",)),
        name="ln_rows",
    )(x, g.reshape(1, d), b.reshape(1, d))


def _in_proj_kernel(x_ref, w_ref, *o_refs, segs):
    xb = x_ref[...].astype(BF16)
    for o_ref, (lo, hi) in zip(o_refs, segs):
        o_ref[...] = jnp.dot(xb, w_ref[:, lo:hi], preferred_element_type=F32)


def _in_proj(x, w, segs, tm):
    n, d = x.shape
    return pl.pallas_call(
        functools.partial(_in_proj_kernel, segs=segs),
        grid=(n // tm,),
        in_specs=[pl.BlockSpec((tm, d), lambda i: (i, 0)),
                  pl.BlockSpec(w.shape, lambda i: (0, 0))],
        out_specs=[pl.BlockSpec((tm, hi - lo), lambda i: (i, 0)) for lo, hi in segs],
        out_shape=[jax.ShapeDtypeStruct((n, hi - lo), F32) for lo, hi in segs],
        compiler_params=_cparams(("parallel",)),
        name="in_proj",
    )(x, w)


def _heads(a, nh):
    nb, c, _ = a.shape
    d = HEAD_DIM
    st = jnp.stack([a[:, :, h * d:(h + 1) * d] for h in range(nh)], axis=1)
    return st.reshape(nb * nh, c, d)


def _unheads(o, nb, nh):
    _, c, d = o.shape
    o4 = o.reshape(nb, nh, c, d)
    return jnp.concatenate([o4[:, h] for h in range(nh)], axis=-1)


def _gated_rms(o, w, z):
    o = o * lax.rsqrt(jnp.mean(o * o, -1, keepdims=True) + RMS_EPS)
    return o * w * _silu(z)


def _mixa_prompt_kernel(h_ref, ab_ref, cw_ref, al_ref, dtb_ref, nw_ref,
                        o_ref, sa_ref, s_scr, prev_scr, *, nb, nh):
    c = pl.program_id(0)
    C, D = CHUNK, HEAD_DIM
    W = nh * D
    G = nb * nh

    @pl.when(c == 0)
    def _():
        s_scr[...] = jnp.zeros_like(s_scr)
        prev_scr[...] = jnp.zeros_like(prev_scr)

    x = h_ref[:, :, 0:3 * W]
    z = h_ref[:, :, 3 * W:4 * W]
    xp = jnp.concatenate([prev_scr[...], x], axis=1)
    cw = cw_ref[...]
    y = (xp[:, 5:5 + C] * cw[0] + xp[:, 6:6 + C] * cw[1]
         + xp[:, 7:7 + C] * cw[2] + xp[:, 8:8 + C] * cw[3])
    prev_scr[...] = x[:, C - 8:C]
    y = _silu(y)

    q = _heads(y[:, :, 0:W], nh)
    k = _heads(y[:, :, W:2 * W], nh)
    v = _heads(y[:, :, 2 * W:3 * W], nh)
    zh = _heads(z, nh)
    q = q * lax.rsqrt(jnp.sum(q * q, -1, keepdims=True) + 1e-6) * (D ** -0.5)
    k = k * lax.rsqrt(jnp.sum(k * k, -1, keepdims=True) + 1e-6)

    ab = ab_ref[...]
    dt = _softplus(ab + dtb_ref[...])
    gfull = -jnp.exp(al_ref[...]) * dt
    bfull = _sigmoid(ab)

    def lane_bc(a, off):
        st = jnp.stack([jnp.broadcast_to(a[:, :, off + h:off + h + 1], (nb, C, D))
                        for h in range(nh)], axis=1)
        return st.reshape(G, C, D)

    gb = lane_bc(gfull, 0)
    beta = lane_bc(bfull, nh)

    ii = lax.broadcasted_iota(jnp.int32, (C, C), 0)
    jj = lax.broadcasted_iota(jnp.int32, (C, C), 1)

    def bcg(m):
        return jnp.broadcast_to(m.astype(F32), (G, C, C))

    gam = _es_mask('gij,gjk->gik', bcg(ii >= jj), gb)
    gam_t = jnp.einsum('gik,gjk->gij', bcg(jj == 0), gam, precision=HI,
                       preferred_element_type=F32)
    dec = jnp.exp(jnp.where(ii >= jj, gam - gam_t, -jnp.inf))

    kb = k * beta
    nmat = jnp.where(ii > jj, _es('gik,gjk->gij', kb, k) * dec, 0.0)
    xinv = (ii == jj).astype(F32) - jnp.where(ii // 2 == jj // 2, nmat, 0.0)
    s = 4
    while s <= C:
        off = jnp.where((ii // s == jj // s) & (ii // (s // 2) != jj // (s // 2)), nmat, 0.0)
        xinv = xinv - _es('gij,gjk->gik', xinv, _es('gij,gjk->gik', off, xinv))
        s *= 2

    egam = jnp.exp(gam)
    rhs = jnp.concatenate([v * beta, kb * egam], axis=-1)
    sol = _es('gij,gjk->gik', xinv, rhs)
    st = s_scr[...]
    u = sol[:, :, 0:D] - _es('gck,gkv->gcv', sol[:, :, D:2 * D], st)
    attn = _es('gik,gjk->gij', q, k) * dec
    o = _es('gck,gkv->gcv', q * egam, st) + _es('gij,gjv->giv', attn, u)
    glast = gam[:, C - 1:C, :]
    kdec = k * jnp.exp(glast - gam)
    st_new = st * jnp.exp(glast) + _es('gck,gcv->gkv', kdec, u)
    s_scr[...] = st_new

    o = _gated_rms(o, nw_ref[...], zh)
    o_ref[...] = _unheads(o, nb, nh)

    @pl.when(c == pl.num_programs(0) - 1)
    def _():
        sa_ref[...] = st_new.reshape(nb, nh, D, D)


def _mixa_prompt(h_a, h_ab, cw, a_log, dt_bias, norm_w, nh):
    nb, t, wa4 = h_a.shape
    C, D = CHUNK, HEAD_DIM
    W = nh * D
    pad = LANES - nh
    al = jnp.pad(a_log, (0, pad)).reshape(1, LANES)
    dtb = jnp.pad(dt_bias, (0, pad)).reshape(1, LANES)
    return pl.pallas_call(
        functools.partial(_mixa_prompt_kernel, nb=nb, nh=nh),
        grid=(t // C,),
        in_specs=[pl.BlockSpec((nb, C, wa4), lambda c: (0, c, 0)),
                  pl.BlockSpec((nb, C, LANES), lambda c: (0, c, 0)),
                  pl.BlockSpec((CONV_W, 3 * W), lambda c: (0, 0)),
                  pl.BlockSpec((1, LANES), lambda c: (0, 0)),
                  pl.BlockSpec((1, LANES), lambda c: (0, 0)),
                  pl.BlockSpec((1, D), lambda c: (0, 0))],
        out_specs=[pl.BlockSpec((nb, C, W), lambda c: (0, c, 0)),
                   pl.BlockSpec((nb, nh, D, D), lambda c: (0, 0, 0, 0))],
        out_shape=[jax.ShapeDtypeStruct((nb, t, W), F32),
                   jax.ShapeDtypeStruct((nb, nh, D, D), F32)],
        scratch_shapes=[pltpu.VMEM((nb * nh, D, D), F32),
                        pltpu.VMEM((nb, 8, 3 * W), F32)],
        compiler_params=_cparams(("arbitrary",)),
        name="mixa_prompt",
    )(h_a, h_ab, cw, al, dtb, norm_w.reshape(1, D))


def _hgrn_prompt_kernel(h_ref, loglb_ref, log1mlb_ref, omlb_ref, nw_ref,
                        o_ref, sc_ref, st_scr, *, nb, nh):
    c = pl.program_id(0)
    C, D = CHUNK, HEAD_DIM
    W = nh * D
    G = nb * nh

    @pl.when(c == 0)
    def _():
        st_scr[...] = jnp.zeros_like(st_scr)

    x = h_ref[...]
    qx = _heads(x[:, :, 0:W], nh)
    fx = _heads(x[:, :, W:2 * W], nh)
    iv = _heads(x[:, :, 2 * W:3 * W], nh)
    zh = _heads(x[:, :, 3 * W:4 * W], nh)

    la = loglb_ref[...]
    lbb = log1mlb_ref[...] - _softplus(-fx)
    mx = jnp.maximum(la, lbb)
    logf = mx + jnp.log1p(jnp.exp(-jnp.abs(la - lbb)))
    kc = omlb_ref[...] * _sigmoid(-fx)
    q = _silu(qx)

    ii = lax.broadcasted_iota(jnp.int32, (C, C), 0)
    jj = lax.broadcasted_iota(jnp.int32, (C, C), 1)
    ri = lax.broadcasted_iota(jnp.int32, (C, D), 0)
    gam = _es_mask('gij,gjk->gik', jnp.broadcast_to((ii >= jj).astype(F32), (G, C, C)), logf)

    def block_starts(hb):
        g4 = gam.reshape(G, C // hb, hb, D)
        starts = g4[:, :, 0:1, :]
        nxt = jnp.concatenate([starts[:, 1:], starts[:, -1:]], axis=1)
        own = jnp.broadcast_to(starts, g4.shape).reshape(G, C, D)
        nx = jnp.broadcast_to(nxt, g4.shape).reshape(G, C, D)
        return own, nx

    attn = jnp.zeros((G, C, C), F32)
    hb = C // 2
    while hb >= 8:
        s = 2 * hb
        own, nx = block_starts(hb)
        upper = (ri % s) >= hb
        qs = jnp.where(upper, q * jnp.exp(gam - own), 0.0)
        ks = jnp.where(upper, 0.0, kc * jnp.exp(jnp.where(upper, 0.0, nx - gam)))
        attn = attn + jnp.where(ii // s == jj // s, _es('gik,gjk->gij', qs, ks), 0.0)
        hb //= 2
    own, _ = block_starts(8)
    qd = q * jnp.exp(gam - own)
    kd = kc * jnp.exp(own - gam)
    attn = attn + jnp.where((ii // 8 == jj // 8) & (ii >= jj), _es('gik,gjk->gij', qd, kd), 0.0)

    st = st_scr[...]
    o = _es('gck,gvk->gcv', q * jnp.exp(gam), st) + _es('gij,gjv->giv', attn, iv)
    glast = gam[:, C - 1:C, :]
    kdec = kc * jnp.exp(glast - gam)
    st_new = st * jnp.exp(glast) + _es('gcv,gck->gvk', iv, kdec)
    st_scr[...] = st_new

    o = _gated_rms(o, nw_ref[...], zh)
    o_ref[...] = _unheads(o, nb, nh)

    @pl.when(c == pl.num_programs(0) - 1)
    def _():
        di = lax.broadcasted_iota(jnp.int32, (D, D), 0)
        dj = lax.broadcasted_iota(jnp.int32, (D, D), 1)
        eye = jnp.broadcast_to((di == dj).astype(F32), (G, D, D))
        sc_ref[...] = _es_mask('gkj,gvj->gkv', eye, st_new).reshape(nb, nh, D, D)


def _hgrn_prompt(h_c, lb, norm_w, nh):
    nb, t, wc4 = h_c.shape
    C, D = CHUNK, HEAD_DIM
    W = nh * D
    G = nb * nh

    def per_group(a):
        return jnp.tile(a.reshape(nh, 1, D), (nb, 1, 1))

    return pl.pallas_call(
        functools.partial(_hgrn_prompt_kernel, nb=nb, nh=nh),
        grid=(t // C,),
        in_specs=[pl.BlockSpec((nb, C, wc4), lambda c: (0, c, 0)),
                  pl.BlockSpec((G, 1, D), lambda c: (0, 0, 0)),
                  pl.BlockSpec((G, 1, D), lambda c: (0, 0, 0)),
                  pl.BlockSpec((G, 1, D), lambda c: (0, 0, 0)),
                  pl.BlockSpec((1, D), lambda c: (0, 0))],
        out_specs=[pl.BlockSpec((nb, C, W), lambda c: (0, c, 0)),
                   pl.BlockSpec((nb, nh, D, D), lambda c: (0, 0, 0, 0))],
        out_shape=[jax.ShapeDtypeStruct((nb, t, W), F32),
                   jax.ShapeDtypeStruct((nb, nh, D, D), F32)],
        scratch_shapes=[pltpu.VMEM((G, D, D), F32)],
        compiler_params=_cparams(("arbitrary",)),
        name="hgrn_prompt",
    )(h_c, per_group(jnp.log(lb)), per_group(jnp.log1p(-lb)), per_group(1.0 - lb),
      norm_w.reshape(1, D))


def _mixa_sample_kernel(al_ref, dtb_ref, xq_ref, xk_ref, xv_ref, z_ref, ab_ref,
                        cwq_ref, cwk_ref, cwv_ref, nw_ref, s0_ref,
                        o_ref, s_ref, kq_scr, *, nh, ts):
    h = pl.program_id(0)
    D = HEAD_DIM
    s_ref[...] = s0_ref[...]
    neg_a = -jnp.exp(al_ref[h])
    dtb = dtb_ref[h]

    def conv(x_ref, cw_ref, t):
        acc = x_ref[t] * cw_ref[0]
        for j in range(1, CONV_W):
            acc = acc + x_ref[t + j] * cw_ref[j]
        return _silu(acc)

    for t in range(ts):
        q = conv(xq_ref, cwq_ref, t)
        k = conv(xk_ref, cwk_ref, t)
        v = conv(xv_ref, cwv_ref, t)
        q = q * lax.rsqrt(jnp.sum(q * q, 0, keepdims=True) + 1e-6) * (D ** -0.5)
        k = k * lax.rsqrt(jnp.sum(k * k, 0, keepdims=True) + 1e-6)
        a = ab_ref[t, pl.ds(h, 1), :]
        b = ab_ref[t, pl.ds(nh + h, 1), :]
        decay = jnp.exp(neg_a * _softplus(a + dtb))
        beta = _sigmoid(b)
        kq_scr[0] = k
        kq_scr[1] = q

        def pass1(d, r):
            kd = kq_scr[0, pl.ds(d, 1), :]
            return r + (s_ref[0, d] * decay) * kd

        r = lax.fori_loop(0, D, pass1, jnp.zeros_like(v))
        vn = beta * (v - r)

        def pass2(d, o):
            kd = kq_scr[0, pl.ds(d, 1), :]
            qd = kq_scr[1, pl.ds(d, 1), :]
            sd = s_ref[0, d] * decay + kd * vn
            s_ref[0, d] = sd
            return o + sd * qd

        o = lax.fori_loop(0, D, pass2, jnp.zeros_like(v))
        o = o * lax.rsqrt(jnp.mean(o * o, 0, keepdims=True) + RMS_EPS)
        o_ref[t] = o * nw_ref[...] * _silu(z_ref[t])


def _mixa_sample(x_t, z_t, ab_t, cw_t, a_log, dt_bias, norm_w, s0_t, nh, ts):
    D = HEAD_DIM
    W = nh * D
    nb = x_t.shape[-1]
    tt = x_t.shape[0]
    nab = ab_t.shape[1]
    smem = pl.BlockSpec(memory_space=pltpu.SMEM)
    return pl.pallas_call(
        functools.partial(_mixa_sample_kernel, nh=nh, ts=ts),
        grid=(nh,),
        in_specs=[smem, smem,
                  pl.BlockSpec((tt, D, nb), lambda h: (0, h, 0)),
                  pl.BlockSpec((tt, D, nb), lambda h: (0, nh + h, 0)),
                  pl.BlockSpec((tt, D, nb), lambda h: (0, 2 * nh + h, 0)),
                  pl.BlockSpec((ts, D, nb), lambda h: (0, h, 0)),
                  pl.BlockSpec((ts, nab, nb), lambda h: (0, 0, 0)),
                  pl.BlockSpec((CONV_W, D, 1), lambda h: (0, h, 0)),
                  pl.BlockSpec((CONV_W, D, 1), lambda h: (0, nh + h, 0)),
                  pl.BlockSpec((CONV_W, D, 1), lambda h: (0, 2 * nh + h, 0)),
                  pl.BlockSpec((D, 1), lambda h: (0, 0)),
                  pl.BlockSpec((1, D, D, nb), lambda h: (h, 0, 0, 0))],
        out_specs=[pl.BlockSpec((ts, D, nb), lambda h: (0, h, 0)),
                   pl.BlockSpec((1, D, D, nb), lambda h: (h, 0, 0, 0))],
        out_shape=[jax.ShapeDtypeStruct((ts, W, nb), F32),
                   jax.ShapeDtypeStruct((nh, D, D, nb), F32)],
        scratch_shapes=[pltpu.VMEM((2, D, nb), F32)],
        compiler_params=_cparams(("arbitrary",)),
        name="mixa_sample",
    )(a_log, dt_bias, x_t, x_t, x_t, z_t, ab_t, cw_t, cw_t, cw_t,
      norm_w.reshape(D, 1), s0_t)


def _hgrn_sample_kernel(q_ref, f_ref, i_ref, z_ref, lb_ref, nw_ref, s0_ref,
                        o_ref, s_ref, kq_scr, *, ts):
    D = HEAD_DIM
    s_ref[...] = s0_ref[...]
    lb = lb_ref[...]
    for t in range(ts):
        fx = f_ref[t]
        f = lb + (1.0 - lb) * _sigmoid(fx)
        kq_scr[0] = f
        kq_scr[1] = (1.0 - lb) * _sigmoid(-fx)
        kq_scr[2] = _silu(q_ref[t])
        v = i_ref[t]

        def body(d, o):
            fd = kq_scr[0, pl.ds(d, 1), :]
            kd = kq_scr[1, pl.ds(d, 1), :]
            qd = kq_scr[2, pl.ds(d, 1), :]
            sd = s_ref[0, d] * fd + kd * v
            s_ref[0, d] = sd
            return o + sd * qd

        o = lax.fori_loop(0, D, body, jnp.zeros_like(v))
        o = o * lax.rsqrt(jnp.mean(o * o, 0, keepdims=True) + RMS_EPS)
        o_ref[t] = o * nw_ref[...] * _silu(z_ref[t])


def _hgrn_sample(hc_t, lb, norm_w, s0_t, nh, ts):
    D = HEAD_DIM
    W = nh * D
    nb = hc_t.shape[-1]
    return pl.pallas_call(
        functools.partial(_hgrn_sample_kernel, ts=ts),
        grid=(nh,),
        in_specs=[pl.BlockSpec((ts, D, nb), lambda h: (0, h, 0)),
                  pl.BlockSpec((ts, D, nb), lambda h: (0, nh + h, 0)),
                  pl.BlockSpec((ts, D, nb), lambda h: (0, 2 * nh + h, 0)),
                  pl.BlockSpec((ts, D, nb), lambda h: (0, 3 * nh + h, 0)),
                  pl.BlockSpec((D, 1), lambda h: (h, 0)),
                  pl.BlockSpec((D, 1), lambda h: (0, 0)),
                  pl.BlockSpec((1, D, D, nb), lambda h: (h, 0, 0, 0))],
        out_specs=[pl.BlockSpec((ts, D, nb), lambda h: (0, h, 0)),
                   pl.BlockSpec((1, D, D, nb), lambda h: (h, 0, 0, 0))],
        out_shape=[jax.ShapeDtypeStruct((ts, W, nb), F32),
                   jax.ShapeDtypeStruct((nh, D, D, nb), F32)],
        scratch_shapes=[pltpu.VMEM((3, D, nb), F32)],
        compiler_params=_cparams(("arbitrary",)),
        name="hgrn_sample",
    )(hc_t, hc_t, hc_t, hc_t, lb.reshape(W, 1), norm_w.reshape(D, 1), s0_t)


def _block_mean_kernel(k_ref, o_ref):
    o_ref[...] = jnp.mean(k_ref[...], axis=0, keepdims=True)


def _block_mean(k):
    nb, t, w = k.shape
    nblk = t // MOBA_BLOCK
    out = pl.pallas_call(
        _block_mean_kernel,
        grid=(nb, nblk),
        in_specs=[pl.BlockSpec((None, MOBA_BLOCK, w), lambda b, n: (b, n, 0))],
        out_specs=pl.BlockSpec((None, None, 1, w), lambda b, n: (b, n, 0, 0)),
        out_shape=jax.ShapeDtypeStruct((nb, nblk, 1, w), F32),
        compiler_params=_cparams(("parallel", "parallel")),
        name="moba_block_mean",
    )(k)
    return out.reshape(nb, nblk, w)


def _alibi_slope(h, nh):
    return float(2.0 ** (-8.0 * (h + 1) / nh))


LOG2E = 1.4426950408889634


def _moba_prompt_kernel(qi_ref, kj_ref, q_ref, k_ref, v_ref, km_ref, o_ref,
                        acc_scr, m_scr, l_scr, sel_scr, qs_scr, bias_scr, *, nh, nblk):
    step = pl.program_id(1)
    qi = qi_ref[step]
    kj = kj_ref[step]
    BS, D = MOBA_BLOCK, HEAD_DIM
    kpos = lax.broadcasted_iota(jnp.int32, (BS, BS), 0)
    qpos = lax.broadcasted_iota(jnp.int32, (BS, BS), 1)

    @pl.when(step == 0)
    def _():
        rel0 = (qpos - kpos).astype(F32)
        for h in range(nh):
            bias_scr[h] = (_alibi_slope(h, nh) * LOG2E) * rel0

    @pl.when(kj == 0)
    def _():
        m_scr[...] = jnp.full_like(m_scr, -jnp.inf)
        l_scr[...] = jnp.zeros_like(l_scr)
        acc_scr[...] = jnp.zeros_like(acc_scr)
        nn = lax.broadcasted_iota(jnp.int32, (nblk, BS), 0)
        elig = nn < qi
        for h in range(nh):
            qh = q_ref[:, h * D:(h + 1) * D]
            qs_scr[h] = (qh * (D ** -0.5 * LOG2E)).astype(BF16)
            gate = _es('nd,qd->nq', km_ref[:, h * D:(h + 1) * D], qh)
            gate = jnp.where(elig, gate, -jnp.inf)
            rank = jnp.zeros((nblk, BS), F32)
            for m in range(nblk):
                gm = gate[m:m + 1, :]
                ahead = (gm > gate) | ((gm == gate) & (nn > m))
                rank = rank + jnp.where(ahead, 1.0, 0.0)
            sel_scr[h] = jnp.where(elig & (rank < MOBA_TOPK), 0.0, -jnp.inf)

    def attend(mask_fn):
        hs = range(nh)
        raw = [jnp.einsum('kd,qd->kq', k_ref[:, h * D:(h + 1) * D].astype(BF16), qs_scr[h],
                          preferred_element_type=F32) for h in hs]
        st = [mask_fn(h, raw[h] - bias_scr[h]) for h in hs]
        m_old = m_scr[...]
        l_old = l_scr[...]
        m_new = jnp.concatenate([jnp.max(st[h], axis=0, keepdims=True) for h in hs], axis=0)
        m_new = jnp.maximum(m_old, m_new)
        m_safe = jnp.where(m_new == -jnp.inf, 0.0, m_new)
        alpha = jnp.exp2(m_old - m_safe)
        p = [jnp.exp2(st[h] - m_safe[h:h + 1, :]) for h in hs]
        l_scr[...] = alpha * l_old + jnp.concatenate(
            [jnp.sum(p[h], axis=0, keepdims=True) for h in hs], axis=0)
        m_scr[...] = m_new
        pv = [jnp.einsum('kd,kq->dq', v_ref[:, h * D:(h + 1) * D].astype(BF16), p[h].astype(BF16),
                         preferred_element_type=F32) for h in hs]
        for h in hs:
            acc_scr[h] = alpha[h:h + 1, :] * acc_scr[h] + pv[h]

    @pl.when(kj < qi)
    def _():
        off = ((qi - kj) * BS).astype(F32)
        attend(lambda h, s: s + (sel_scr[h, pl.ds(kj, 1), :] - (_alibi_slope(h, nh) * LOG2E) * off))

    @pl.when(kj == qi)
    def _():
        attend(lambda h, s: jnp.where(qpos >= kpos, s, -jnp.inf))
        outs = [acc_scr[h] / l_scr[h:h + 1, :] for h in range(nh)]
        o_ref[...] = jnp.concatenate(outs, axis=0).T


def _moba_prompt(q, k, v, nh):
    nb, t, w = q.shape
    BS, D = MOBA_BLOCK, HEAD_DIM
    nblk = t // BS
    km = _block_mean(k)
    pairs = [(i, j) for i in range(nblk) for j in range(i + 1)]
    qi = jnp.asarray([p[0] for p in pairs], jnp.int32)
    kj = jnp.asarray([p[1] for p in pairs], jnp.int32)
    grid_spec = pltpu.PrefetchScalarGridSpec(
        num_scalar_prefetch=2,
        grid=(nb, len(pairs)),
        in_specs=[pl.BlockSpec((None, BS, w), lambda b, s, qi, kj: (b, qi[s], 0)),
                  pl.BlockSpec((None, BS, w), lambda b, s, qi, kj: (b, kj[s], 0)),
                  pl.BlockSpec((None, BS, w), lambda b, s, qi, kj: (b, kj[s], 0)),
                  pl.BlockSpec((None, nblk, w), lambda b, s, qi, kj: (b, 0, 0))],
        out_specs=pl.BlockSpec((None, BS, w), lambda b, s, qi, kj: (b, qi[s], 0)),
        scratch_shapes=[pltpu.VMEM((nh, D, BS), F32),
                        pltpu.VMEM((nh, BS), F32),
                        pltpu.VMEM((nh, BS), F32),
                        pltpu.VMEM((nh, nblk, BS), F32),
                        pltpu.VMEM((nh, BS, D), BF16),
                        pltpu.VMEM((nh, BS, BS), F32)])
    return pl.pallas_call(
        functools.partial(_moba_prompt_kernel, nh=nh, nblk=nblk),
        grid_spec=grid_spec,
        out_shape=jax.ShapeDtypeStruct((nb, t, w), F32),
        compiler_params=_cparams(("parallel", "arbitrary")),
        name="moba_prompt",
    )(qi, kj, q, k, v, km)


def _moba_sample_kernel(pt_ref, q_ref, kn_ref, vn_ref, *refs, nh, ts, past_len, bps):
    npg = 2 * bps
    k_refs, v_refs = refs[:npg], refs[npg:2 * npg]
    o_ref, m_scr, l_scr, g_scr, acc_scr = refs[2 * npg:]
    step = pl.program_id(1)
    BS, D = MOBA_BLOCK, HEAD_DIM
    page = BS // 2
    TP = q_ref.shape[1]
    scale = D ** -0.5
    q = q_ref[...]
    qb = q.astype(BF16)

    tpos = lax.broadcasted_iota(jnp.int32, (nh, TP, 1), 1).astype(F32)
    hidx = lax.broadcasted_iota(jnp.int32, (nh, TP, 1), 0)
    slope = jnp.zeros((nh, TP, 1), F32)
    for h in range(nh):
        slope = jnp.where(hidx == h, _alibi_slope(h, nh), slope)
    kofs = lax.broadcasted_iota(jnp.int32, (nh, TP, page), 2).astype(F32)
    qpos = past_len + tpos

    pages = range(npg)
    raw = [jnp.einsum('htd,hdk->htk', qb, k_refs[i][...].astype(BF16), preferred_element_type=F32)
           for i in pages]
    scores = []
    for i in pages:
        kpos = ((step * bps + i // 2) * BS + (i % 2) * page).astype(F32) + kofs
        scores.append(raw[i] * scale - slope * (qpos - kpos))
    m_blk = [jnp.maximum(jnp.max(scores[2 * b], axis=-1, keepdims=True),
                         jnp.max(scores[2 * b + 1], axis=-1, keepdims=True)) for b in range(bps)]
    p = [jnp.exp(scores[i] - m_blk[i // 2]) for i in pages]
    pv = [jnp.einsum('htk,hdk->htd', p[i].astype(BF16), v_refs[i][...].astype(BF16),
                     preferred_element_type=F32) for i in pages]
    for b in range(bps):
        n = step * bps + b
        l_n = (jnp.sum(p[2 * b], axis=-1, keepdims=True)
               + jnp.sum(p[2 * b + 1], axis=-1, keepdims=True))
        gate = (jnp.sum(raw[2 * b], axis=-1, keepdims=True)
                + jnp.sum(raw[2 * b + 1], axis=-1, keepdims=True)) * (1.0 / BS)
        m_scr[n] = jnp.broadcast_to(m_blk[b], (nh, TP, LANES))
        l_scr[n] = jnp.broadcast_to(l_n, (nh, TP, LANES))
        g_scr[n] = jnp.broadcast_to(gate, (nh, TP, LANES))
        acc_scr[n] = pv[2 * b] + pv[2 * b + 1]

    @pl.when(step == pl.num_programs(1) - 1)
    def _():
        nb_past = past_len // BS
        gates = [g_scr[i][:, :, 0:1] for i in range(nb_past)]
        sels = []
        for i in range(nb_past):
            rank = jnp.zeros((nh, TP, 1), F32)
            for m in range(nb_past):
                if m == i:
                    continue
                ahead = (gates[m] > gates[i]) if m > i else (gates[m] >= gates[i])
                rank = rank + jnp.where(ahead, 1.0, 0.0)
            sels.append(rank < MOBA_TOPK)
        kn = kn_ref[...]
        vn = vn_ref[...]
        s_own = []
        for j in range(ts):
            sj = jnp.sum(q * kn[:, j:j + 1, :], axis=-1, keepdims=True) * scale
            sj = sj - slope * (tpos - j)
            s_own.append(jnp.where(tpos >= j, sj, -jnp.inf))
        m_tot = s_own[0]
        for j in range(1, ts):
            m_tot = jnp.maximum(m_tot, s_own[j])
        for i in range(nb_past):
            m_tot = jnp.maximum(m_tot, jnp.where(sels[i], m_scr[i][:, :, 0:1], -jnp.inf))
        num = jnp.zeros((nh, TP, D), F32)
        den = jnp.zeros((nh, TP, 1), F32)
        for j in range(ts):
            pj = jnp.exp(s_own[j] - m_tot)
            num = num + pj * vn[:, j:j + 1, :]
            den = den + pj
        for i in range(nb_past):
            wi = jnp.where(sels[i], jnp.exp(m_scr[i][:, :, 0:1] - m_tot), 0.0)
            num = num + wi * acc_scr[i]
            den = den + wi * l_scr[i][:, :, 0:1]
        o_ref[...] = num / den


def _moba_sample(q, kn, vn, cache_kt, cache_vt, page_table, layer, ts):
    nb, nh, TP, D = q.shape
    page = cache_kt.shape[4]
    n_pages = page_table.shape[1]
    past_len = n_pages * page
    BS = MOBA_BLOCK
    assert BS == 2 * page and past_len % BS == 0 and ts <= TP
    nblk = past_len // BS
    bps = next(c for c in (4, 2, 1) if nblk % c == 0)
    npg = 2 * bps
    pt = page_table.reshape(-1)

    def page_spec(i):
        return pl.BlockSpec((None, None, nh, D, page),
                            lambda b, s, pt: (layer, pt[b * n_pages + s * npg + i], 0, 0, 0))

    tok = pl.BlockSpec((None, nh, TP, D), lambda b, s, pt: (b, 0, 0, 0))
    pages = [page_spec(i) for i in range(npg)]
    grid_spec = pltpu.PrefetchScalarGridSpec(
        num_scalar_prefetch=1,
        grid=(nb, nblk // bps),
        in_specs=[tok, tok, tok] + pages + pages,
        out_specs=tok,
        scratch_shapes=[pltpu.VMEM((nblk, nh, TP, LANES), F32),
                        pltpu.VMEM((nblk, nh, TP, LANES), F32),
                        pltpu.VMEM((nblk, nh, TP, LANES), F32),
                        pltpu.VMEM((nblk, nh, TP, D), F32)])
    return pl.pallas_call(
        functools.partial(_moba_sample_kernel, nh=nh, ts=ts, past_len=past_len, bps=bps),
        grid_spec=grid_spec,
        out_shape=jax.ShapeDtypeStruct((nb, nh, TP, D), F32),
        compiler_params=_cparams(("parallel", "arbitrary")),
        name="moba_sample",
    )(pt, q, kn, vn, *([cache_kt] * npg), *([cache_vt] * npg))


def _router_gates(y, r, n_exp):
    logits = _es('mk,kn->mn', y, r)
    lane = lax.broadcasted_iota(jnp.int32, logits.shape, 1)
    logits = jnp.where(lane < n_exp, logits, -jnp.inf)
    m1 = jnp.max(logits, -1, keepdims=True)
    i1 = jnp.min(jnp.where(logits == m1, lane, LANES), -1, keepdims=True)
    first = lane == i1
    rest = jnp.where(first, -jnp.inf, logits)
    m2 = jnp.max(rest, -1, keepdims=True)
    i2 = jnp.min(jnp.where(rest == m2, lane, LANES), -1, keepdims=True)
    second = lane == i2
    e = jnp.exp(m2 - m1)
    den = 1.0 + e
    return jnp.where(first, 1.0 / den, 0.0) + jnp.where(second, e / den, 0.0)


def _out_proj_kernel(*refs, alpha, wa, wb, n_exp):
    if n_exp:
        x_ref, oa_ref, ob_ref, oc_ref, w_ref, g_ref, b_ref, r_ref, o_ref, gate_ref = refs
    else:
        x_ref, oa_ref, ob_ref, oc_ref, w_ref, g_ref, b_ref, o_ref = refs
    acc = jnp.dot(oa_ref[...].astype(BF16), w_ref[0:wa, :], preferred_element_type=F32)
    acc = acc + jnp.dot(ob_ref[...].astype(BF16), w_ref[wa:wa + wb, :], preferred_element_type=F32)
    acc = acc + jnp.dot(oc_ref[...].astype(BF16), w_ref[wa + wb:, :], preferred_element_type=F32)
    y = _layer_norm(alpha * x_ref[...] + acc, g_ref[...], b_ref[...])
    o_ref[...] = y
    if n_exp:
        gate_ref[...] = _router_gates(y, r_ref[...], n_exp)


def _out_proj(x, oa, ob, oc, w, g, b, router, alpha, tm):
    n, d = x.shape
    wa, wb, wc = oa.shape[1], ob.shape[1], oc.shape[1]
    n_exp = 0 if router is None else router.shape[1]
    row = lambda width: pl.BlockSpec((tm, width), lambda i: (i, 0))
    full = lambda shape: pl.BlockSpec(shape, lambda i: (0, 0))
    in_specs = [row(d), row(wa), row(wb), row(wc), full(w.shape), full((1, d)), full((1, d))]
    args = [x, oa, ob, oc, w, g.reshape(1, d), b.reshape(1, d)]
    out_specs = [row(d)]
    out_shape = [jax.ShapeDtypeStruct((n, d), F32)]
    if n_exp:
        in_specs.append(full((d, LANES)))
        args.append(jnp.pad(router, ((0, 0), (0, LANES - n_exp))))
        out_specs.append(row(LANES))
        out_shape.append(jax.ShapeDtypeStruct((n, LANES), F32))
    res = pl.pallas_call(
        functools.partial(_out_proj_kernel, alpha=alpha, wa=wa, wb=wb, n_exp=n_exp),
        grid=(n // tm,),
        in_specs=in_specs, out_specs=out_specs, out_shape=out_shape,
        compiler_params=_cparams(("parallel",)),
        name="out_proj_ln",
    )(*args)
    return (res[0], res[1]) if n_exp else (res[0], None)


def _ffn_kernel(x_ref, w1_ref, w3_ref, w2_ref, g_ref, b_ref, o_ref, acc_scr, xb_scr, *, alpha):
    f = pl.program_id(1)

    @pl.when(f == 0)
    def _():
        acc_scr[...] = jnp.zeros_like(acc_scr)
        xb_scr[...] = x_ref[...].astype(BF16)

    xb = xb_scr[...]
    h1 = jnp.dot(xb, w1_ref[...].astype(BF16), preferred_element_type=F32)
    h3 = jnp.dot(xb, w3_ref[...].astype(BF16), preferred_element_type=F32)
    hid = (_silu(h1) * h3).astype(BF16)
    acc_scr[...] += jnp.dot(hid, w2_ref[...].astype(BF16), preferred_element_type=F32)

    @pl.when(f == pl.num_programs(1) - 1)
    def _():
        o_ref[...] = _layer_norm(alpha * x_ref[...] + acc_scr[...], g_ref[...], b_ref[...])


def _ffn_dense(x, w1, w3, w2, g, b, alpha, tm, tf):
    n, d = x.shape
    dff = w1.shape[1]
    return pl.pallas_call(
        functools.partial(_ffn_kernel, alpha=alpha),
        grid=(n // tm, dff // tf),
        in_specs=[pl.BlockSpec((tm, d), lambda i, f: (i, 0)),
                  pl.BlockSpec((d, tf), lambda i, f: (0, f)),
                  pl.BlockSpec((d, tf), lambda i, f: (0, f)),
                  pl.BlockSpec((tf, d), lambda i, f: (f, 0)),
                  pl.BlockSpec((1, d), lambda i, f: (0, 0)),
                  pl.BlockSpec((1, d), lambda i, f: (0, 0))],
        out_specs=pl.BlockSpec((tm, d), lambda i, f: (i, 0)),
        out_shape=jax.ShapeDtypeStruct((n, d), F32),
        scratch_shapes=[pltpu.VMEM((tm, d), F32), pltpu.VMEM((tm, d), BF16)],
        compiler_params=_cparams(("parallel", "arbitrary")),
        name="ffn_dense",
    )(x, w1, w3, w2, g.reshape(1, d), b.reshape(1, d))


def _moe_kernel(x_ref, gate_ref, w1_ref, w3_ref, w2_ref, g_ref, b_ref, o_ref,
                acc_scr, xb_scr, xc_scr, yc_scr, posc_scr, post_scr, selt_scr, cnt_smem,
                *, alpha, ch, nch):
    e = pl.program_id(1)
    f = pl.program_id(2)
    first_f = f == 0
    last_f = f == pl.num_programs(2) - 1
    tm = x_ref.shape[0]

    @pl.when((e == 0) & first_f)
    def _():
        acc_scr[...] = jnp.zeros_like(acc_scr)
        xb_scr[...] = x_ref[...].astype(BF16)
        sel = jnp.where(gate_ref[...] > 0.0, 1.0, 0.0)
        ri = lax.broadcasted_iota(jnp.int32, (tm, tm), 0)
        ci = lax.broadcasted_iota(jnp.int32, (tm, tm), 1)
        before = jnp.where(ci < ri, 1.0, 0.0).astype(BF16)
        pos = jnp.dot(before, sel.astype(BF16), preferred_element_type=F32)
        posc_scr[...] = pos
        post_scr[...] = pos.T
        selt_scr[...] = sel.T

    @pl.when(first_f)
    def _():
        lane = lax.broadcasted_iota(jnp.int32, (tm, LANES), 1)
        cnt = jnp.sum(jnp.where((lane == e) & (gate_ref[...] > 0.0), 1.0, 0.0))
        cnt_smem[0] = cnt.astype(jnp.int32)
        prow = post_scr[pl.ds(e, 1), :]
        srow = selt_scr[pl.ds(e, 1), :]
        for k in range(nch):
            @pl.when(cnt_smem[0] > k * ch)
            def _():
                r = (lax.broadcasted_iota(jnp.int32, (ch, tm), 0) + k * ch).astype(F32)
                pk = jnp.where((prow == r) & (srow > 0.0), 1.0, 0.0).astype(BF16)
                xc_scr[k] = jnp.dot(pk, xb_scr[...], preferred_element_type=F32).astype(BF16)

    cnt = cnt_smem[0]
    for k in range(nch):
        @pl.when(cnt > k * ch)
        def _():
            xk = xc_scr[k]
            h1 = jnp.dot(xk, w1_ref[...].astype(BF16), preferred_element_type=F32)
            h3 = jnp.dot(xk, w3_ref[...].astype(BF16), preferred_element_type=F32)
            hid = (_silu(h1) * h3).astype(BF16)
            part = jnp.dot(hid, w2_ref[...].astype(BF16), preferred_element_type=F32)

            @pl.when(first_f)
            def _():
                yc_scr[k] = part

            @pl.when(jnp.logical_not(first_f))
            def _():
                yc_scr[k] += part

    @pl.when(last_f)
    def _():
        lane = lax.broadcasted_iota(jnp.int32, (tm, LANES), 1)
        mine = lane == e
        gcol = jnp.sum(jnp.where(mine, gate_ref[...], 0.0), -1, keepdims=True)
        pcol = jnp.sum(jnp.where(mine, posc_scr[...], 0.0), -1, keepdims=True)
        for k in range(nch):
            @pl.when(cnt > k * ch)
            def _():
                r = (lax.broadcasted_iota(jnp.int32, (tm, ch), 1) + k * ch).astype(F32)
                ptk = jnp.where((pcol == r) & (gcol > 0.0), 1.0, 0.0).astype(BF16)
                acc_scr[...] += gcol * jnp.dot(ptk, yc_scr[k].astype(BF16),
                                               preferred_element_type=F32)

    @pl.when((e == pl.num_programs(1) - 1) & last_f)
    def _():
        o_ref[...] = _layer_norm(alpha * x_ref[...] + acc_scr[...], g_ref[...], b_ref[...])


def _ffn_moe(x, gate, w1, w3, w2, li, g, b, alpha, tm, tf):
    n, d = x.shape
    _, n_exp, _, dff = w1.shape
    ch = min(tm, -(-(tm * MOE_TOPK * 3) // (n_exp * 2 * LANES)) * LANES)
    nch = -(-tm // ch)
    return pl.pallas_call(
        functools.partial(_moe_kernel, alpha=alpha, ch=ch, nch=nch),
        grid=(n // tm, n_exp, dff // tf),
        in_specs=[pl.BlockSpec((tm, d), lambda i, e, f: (i, 0)),
                  pl.BlockSpec((tm, LANES), lambda i, e, f: (i, 0)),
                  pl.BlockSpec((None, None, d, tf), lambda i, e, f: (li, e, 0, f)),
                  pl.BlockSpec((None, None, d, tf), lambda i, e, f: (li, e, 0, f)),
                  pl.BlockSpec((None, None, tf, d), lambda i, e, f: (li, e, f, 0)),
                  pl.BlockSpec((1, d), lambda i, e, f: (0, 0)),
                  pl.BlockSpec((1, d), lambda i, e, f: (0, 0))],
        out_specs=pl.BlockSpec((tm, d), lambda i, e, f: (i, 0)),
        out_shape=jax.ShapeDtypeStruct((n, d), F32),
        scratch_shapes=[pltpu.VMEM((tm, d), F32), pltpu.VMEM((tm, d), BF16),
                        pltpu.VMEM((nch, ch, d), BF16), pltpu.VMEM((nch, ch, d), F32),
                        pltpu.VMEM((tm, LANES), F32), pltpu.VMEM((LANES, tm), F32),
                        pltpu.VMEM((LANES, tm), F32), pltpu.SMEM((1,), jnp.int32)],
        compiler_params=_cparams(("parallel", "arbitrary", "arbitrary")),
        name="ffn_moe",
    )(x, gate, w1, w3, w2, g.reshape(1, d), b.reshape(1, d))


def _row_tile(n, cap):
    t = cap
    while n % t:
        t //= 2
    return t


def _ff_tile(dff):
    for t in (512, 256, 128):
        if dff % t == 0:
            return t
    return dff


def kernel(x_prompt, x_sample, cache_k, cache_v, page_table, state_a, state_a_conv, state_c, ln_in_g, ln_in_b, w_in, w_out, a_conv_w, a_a_log, a_dt_bias, a_norm_w, c_lb_logits, c_norm_w, ln1_g, ln1_b, ln2_g, ln2_b, ffn_w1, ffn_w3, ffn_w2, moe_router, moe_w1, moe_w3, moe_w2):
    bp, tp, d = x_prompt.shape
    bs, ts, _ = x_sample.shape
    depth = w_in.shape[0]
    D = HEAD_DIM
    wa, wb = d // 4, d // 2
    wc = d - wa - wb
    ha, hb, hc = wa // D, wb // D, wc // D
    kc = hc * D
    alpha = (2.0 * depth) ** 0.25
    off_b = 4 * wa + 2 * ha
    off_c = off_b + 3 * wb
    p_in = off_c + 2 * kc + 2 * wc
    assert w_in.shape[2] == p_in and 2 * ha <= LANES

    seg_w = [4 * wa, wb, wb, wb, 2 * kc + 2 * wc, LANES]
    segs, lo = [], 0
    for sw in seg_w:
        segs.append((lo, lo + sw))
        lo += sw

    lb_all = jnp.cumsum(jax.nn.softmax(c_lb_logits.astype(F32), axis=0), axis=0)
    lb_all = lb_all - lb_all[0:1]

    n_pool, page = cache_k.shape[1], cache_k.shape[2]
    ck = cache_k.transpose(0, 1, 3, 4, 2)
    cv = cache_v.transpose(0, 1, 3, 4, 2)
    tp_pad = -(-ts // 8) * 8

    np_rows, ns_rows = bp * tp, bs * ts
    tm_p = _row_tile(np_rows, 256)
    tm_s = _row_tile(ns_rows, 256)
    xp = _ln_rows(x_prompt.reshape(np_rows, d), ln_in_g, ln_in_b, tm_p)
    xs = _ln_rows(x_sample.reshape(ns_rows, d), ln_in_g, ln_in_b, tm_s)

    outs = {k: [] for k in ("kp", "vp", "ks", "vs", "sap", "sas", "cvp", "cvs", "scp", "scs")}
    for l in range(depth):
        wl = w_in[l]
        w_perm = jnp.concatenate(
            [wl[:, :4 * wa], wl[:, off_b:], wl[:, 4 * wa:off_b],
             jnp.zeros((d, LANES - 2 * ha), wl.dtype)], axis=1).astype(BF16)
        w_o = w_out[l].astype(BF16)
        cw = a_conv_w[l]

        h_a, q_b, k_b, v_b, h_c, h_ab = _in_proj(xp, w_perm, segs, tm_p)
        h_a3 = h_a.reshape(bp, tp, 4 * wa)
        oa, sa_new = _mixa_prompt(h_a3, h_ab.reshape(bp, tp, LANES), cw, a_a_log[l], a_dt_bias[l],
                                  a_norm_w[l], ha)
        ob = _moba_prompt(q_b.reshape(bp, tp, wb), k_b.reshape(bp, tp, wb), v_b.reshape(bp, tp, wb), hb)
        oc, sc_new = _hgrn_prompt(h_c.reshape(bp, tp, 4 * wc), lb_all[l], c_norm_w[l], hc)
        outs["kp"].append(k_b.reshape(bp, tp, hb, D))
        outs["vp"].append(v_b.reshape(bp, tp, hb, D))
        outs["sap"].append(sa_new)
        conv_full = jnp.concatenate([jnp.zeros((bp, CONV_W - 1, 3 * wa), F32), h_a3[:, :, :3 * wa]], axis=1) \
            if tp < CONV_W - 1 else h_a3[:, :, :3 * wa]
        outs["cvp"].append(conv_full[:, conv_full.shape[1] - (CONV_W - 1):])
        outs["scp"].append(sc_new)
        router = moe_router[l // 2] if l % 2 else None
        x1p, gate_p = _out_proj(xp, oa.reshape(np_rows, wa), ob.reshape(np_rows, wb),
                                oc.reshape(np_rows, wc), w_o, ln1_g[l], ln1_b[l], router, alpha, tm_p)

        h_a, q_b, k_b, v_b, h_c, h_ab = _in_proj(xs, w_perm, segs, tm_s)
        h_a3 = h_a.reshape(bs, ts, 4 * wa)
        x_full = jnp.concatenate([state_a_conv[l].astype(F32), h_a3[:, :, :3 * wa]], axis=1)
        x_t = x_full.transpose(1, 2, 0)
        z_t = h_a3[:, :, 3 * wa:].transpose(1, 2, 0)
        ab_t = h_ab.reshape(bs, ts, LANES)[:, :, :max(8, 2 * ha)].transpose(1, 2, 0)
        s0_t = state_a[l].astype(F32).transpose(1, 2, 3, 0)
        oa_t, sa_t = _mixa_sample(x_t, z_t, ab_t, cw.reshape(CONV_W, 3 * wa, 1), a_a_log[l], a_dt_bias[l],
                                  a_norm_w[l], s0_t, ha, ts)
        oa = oa_t.transpose(2, 0, 1).reshape(ns_rows, wa)
        outs["sas"].append(sa_t.transpose(3, 0, 1, 2))
        outs["cvs"].append(x_full[:, x_full.shape[1] - (CONV_W - 1):])
        def head_major(a):
            a = a.reshape(bs, ts, hb, D).transpose(0, 2, 1, 3)
            return jnp.pad(a, ((0, 0), (0, 0), (0, tp_pad - ts), (0, 0)))

        ob = _moba_sample(head_major(q_b), head_major(k_b), head_major(v_b), ck, cv, page_table, l, ts)
        ob = ob[:, :, :ts].transpose(0, 2, 1, 3)
        outs["ks"].append(k_b.reshape(bs, ts, hb, D))
        outs["vs"].append(v_b.reshape(bs, ts, hb, D))
        hc_t = h_c.reshape(bs, ts, 4 * wc).transpose(1, 2, 0)
        sc0_t = state_c[l].astype(F32).transpose(1, 2, 3, 0)
        oc_t, sc_t = _hgrn_sample(hc_t, lb_all[l], c_norm_w[l], sc0_t, hc, ts)
        oc = oc_t.transpose(2, 0, 1).reshape(ns_rows, wc)
        outs["scs"].append(sc_t.transpose(3, 0, 1, 2))
        x1s, gate_s = _out_proj(xs, oa, ob.reshape(ns_rows, wb), oc, w_o, ln1_g[l], ln1_b[l], router,
                                alpha, tm_s)

        if l % 2 == 0:
            w1, w3, w2 = ffn_w1[l // 2].astype(BF16), ffn_w3[l // 2].astype(BF16), ffn_w2[l // 2].astype(BF16)
            tf = _ff_tile(w1.shape[1])
            xp = _ffn_dense(x1p, w1, w3, w2, ln2_g[l], ln2_b[l], alpha, _row_tile(np_rows, 1024), tf)
            xs = _ffn_dense(x1s, w1, w3, w2, ln2_g[l], ln2_b[l], alpha, _row_tile(ns_rows, 1024), tf)
        else:
            tf = _ff_tile(moe_w1.shape[3])
            xp = _ffn_moe(x1p, gate_p, moe_w1, moe_w3, moe_w2, l // 2, ln2_g[l], ln2_b[l], alpha,
                          _row_tile(np_rows, 1024), tf)
            xs = _ffn_moe(x1s, gate_s, moe_w1, moe_w3, moe_w2, l // 2, ln2_g[l], ln2_b[l], alpha,
                          _row_tile(ns_rows, 1024), tf)

    st = lambda key: jnp.stack(outs[key])
    return (xp.reshape(bp, tp, d), xs.reshape(bs, ts, d), st("kp"), st("vp"), st("ks"), st("vs"),
            st("sap"), st("sas"), st("cvp"), st("cvs"), st("scp"), st("scs"))
```

```python
import functools
import math

import numpy as np
import jax
import jax.numpy as jnp
from jax import lax
from jax.experimental import pallas as pl
from jax.experimental.pallas import tpu as pltpu

HEAD_DIM = 64
CONV_W = 4
CHUNK = 64
MOBA_BLOCK = 256
MOBA_TOPK = 3
MOE_TOPK = 2
LN_EPS = 1e-5
RMS_EPS = 1e-6
LANES = 128
VMEM_LIMIT = 56 * 1024 * 1024
HI = lax.Precision.HIGHEST
F32 = jnp.float32
BF16 = jnp.bfloat16


def _cparams(sem):
    return pltpu.CompilerParams(dimension_semantics=sem, vmem_limit_bytes=VMEM_LIMIT)


def _bdot(a, b):
    return jnp.dot(a.astype(BF16), b.astype(BF16), preferred_element_type=F32)


def _split2(a):
    hi = a.astype(BF16)
    return hi, (a - hi.astype(F32)).astype(BF16)


def _es(spec, a, b):
    ah, al = _split2(a)
    bh, bl = _split2(b)
    e = functools.partial(jnp.einsum, spec, preferred_element_type=F32)
    return e(ah, bh) + (e(ah, bl) + e(al, bh))


def _es_mask(spec, m, b):
    bh = b.astype(BF16)
    r = b - bh.astype(F32)
    bm = r.astype(BF16)
    bl = (r - bm.astype(F32)).astype(BF16)
    mb = m.astype(BF16)
    e = functools.partial(jnp.einsum, spec, preferred_element_type=F32)
    return e(mb, bh) + (e(mb, bm) + e(mb, bl))


def _layer_norm(x, g, b):
    mu = jnp.mean(x, -1, keepdims=True)
    xc = x - mu
    var = jnp.mean(xc * xc, -1, keepdims=True)
    return xc * lax.rsqrt(var + LN_EPS) * g + b


def _sigmoid(x):
    return 1.0 / (1.0 + jnp.exp(-x))


def _silu(x):
    return x * _sigmoid(x)


def _softplus(x):
    return jnp.maximum(x, 0.0) + jnp.log1p(jnp.exp(-jnp.abs(x)))


def _ln_kernel(x_ref, g_ref, b_ref, o_ref):
    o_ref[...] = _layer_norm(x_ref[...], g_ref[...], b_ref[...])


def _ln_rows(x, g, b, tm):
    n, d = x.shape
    return pl.pallas_call(
        _ln_kernel,
        grid=(n // tm,),
        in_specs=[pl.BlockSpec((tm, d), lambda i: (i, 0)),
                  pl.BlockSpec((1, d), lambda i: (0, 0)),
                  pl.BlockSpec((1, d), lambda i: (0, 0))],
        out_specs=pl.BlockSpec((tm, d), lambda i: (i, 0)),
        out_shape=jax.ShapeDtypeStruct((n, d), F32),
        compiler_params=_cparams(("parallel",)),
        name="ln_rows",
    )(x, g.reshape(1, d), b.reshape(1, d))


def _in_proj_kernel(x_ref, w_ref, *o_refs, segs):
    xb = x_ref[...].astype(BF16)
    for o_ref, (lo, hi) in zip(o_refs, segs):
        o_ref[...] = jnp.dot(xb, w_ref[:, lo:hi], preferred_element_type=F32)


def _in_proj(x, w, segs, tm):
    n, d = x.shape
    return pl.pallas_call(
        functools.partial(_in_proj_kernel, segs=segs),
        grid=(n // tm,),
        in_specs=[pl.BlockSpec((tm, d), lambda i: (i, 0)),
                  pl.BlockSpec(w.shape, lambda i: (0, 0))],
        out_specs=[pl.BlockSpec((tm, hi - lo), lambda i: (i, 0)) for lo, hi in segs],
        out_shape=[jax.ShapeDtypeStruct((n, hi - lo), F32) for lo, hi in segs],
        compiler_params=_cparams(("parallel",)),
        name="in_proj",
    )(x, w)


def _heads(a, nh):
    nb, c, _ = a.shape
    d = HEAD_DIM
    st = jnp.stack([a[:, :, h * d:(h + 1) * d] for h in range(nh)], axis=1)
    return st.reshape(nb * nh, c, d)


def _unheads(o, nb, nh):
    _, c, d = o.shape
    o4 = o.reshape(nb, nh, c, d)
    return jnp.concatenate([o4[:, h] for h in range(nh)], axis=-1)


def _gated_rms(o, w, z):
    o = o * lax.rsqrt(jnp.mean(o * o, -1, keepdims=True) + RMS_EPS)
    return o * w * _silu(z)


def _mixa_prompt_kernel(h_ref, ab_ref, cw_ref, al_ref, dtb_ref, nw_ref,
                        o_ref, sa_ref, s_scr, prev_scr, *, nb, nh):
    c = pl.program_id(0)
    C, D = CHUNK, HEAD_DIM
    W = nh * D
    G = nb * nh

    @pl.when(c == 0)
    def _():
        s_scr[...] = jnp.zeros_like(s_scr)
        prev_scr[...] = jnp.zeros_like(prev_scr)

    x = h_ref[:, :, 0:3 * W]
    z = h_ref[:, :, 3 * W:4 * W]
    xp = jnp.concatenate([prev_scr[...], x], axis=1)
    cw = cw_ref[...]
    y = (xp[:, 5:5 + C] * cw[0] + xp[:, 6:6 + C] * cw[1]
         + xp[:, 7:7 + C] * cw[2] + xp[:, 8:8 + C] * cw[3])
    prev_scr[...] = x[:, C - 8:C]
    y = _silu(y)

    q = _heads(y[:, :, 0:W], nh)
    k = _heads(y[:, :, W:2 * W], nh)
    v = _heads(y[:, :, 2 * W:3 * W], nh)
    zh = _heads(z, nh)
    q = q * lax.rsqrt(jnp.sum(q * q, -1, keepdims=True) + 1e-6) * (D ** -0.5)
    k = k * lax.rsqrt(jnp.sum(k * k, -1, keepdims=True) + 1e-6)

    ab = ab_ref[...]
    dt = _softplus(ab + dtb_ref[...])
    gfull = -jnp.exp(al_ref[...]) * dt
    bfull = _sigmoid(ab)

    def lane_bc(a, off):
        st = jnp.stack([jnp.broadcast_to(a[:, :, off + h:off + h + 1], (nb, C, D))
                        for h in range(nh)], axis=1)
        return st.reshape(G, C, D)

    gb = lane_bc(gfull, 0)
    beta = lane_bc(bfull, nh)

    ii = lax.broadcasted_iota(jnp.int32, (C, C), 0)
    jj = lax.broadcasted_iota(jnp.int32, (C, C), 1)

    def bcg(m):
        return jnp.broadcast_to(m.astype(F32), (G, C, C))

    gam = _es_mask('gij,gjk->gik', bcg(ii >= jj), gb)
    gam_t = jnp.einsum('gik,gjk->gij', bcg(jj == 0), gam, precision=HI,
                       preferred_element_type=F32)
    dec = jnp.exp(jnp.where(ii >= jj, gam - gam_t, -jnp.inf))

    kb = k * beta
    nmat = jnp.where(ii > jj, _es('gik,gjk->gij', kb, k) * dec, 0.0)
    xinv = (ii == jj).astype(F32) - jnp.where(ii // 2 == jj // 2, nmat, 0.0)
    s = 4
    while s <= C:
        off = jnp.where((ii // s == jj // s) & (ii // (s // 2) != jj // (s // 2)), nmat, 0.0)
        xinv = xinv - _es('gij,gjk->gik', xinv, _es('gij,gjk->gik', off, xinv))
        s *= 2

    egam = jnp.exp(gam)
    rhs = jnp.concatenate([v * beta, kb * egam], axis=-1)
    sol = _es('gij,gjk->gik', xinv, rhs)
    st = s_scr[...]
    u = sol[:, :, 0:D] - _es('gck,gkv->gcv', sol[:, :, D:2 * D], st)
    attn = _es('gik,gjk->gij', q, k) * dec
    o = _es('gck,gkv->gcv', q * egam, st) + _es('gij,gjv->giv', attn, u)
    glast = gam[:, C - 1:C, :]
    kdec = k * jnp.exp(glast - gam)
    st_new = st * jnp.exp(glast) + _es('gck,gcv->gkv', kdec, u)
    s_scr[...] = st_new

    o = _gated_rms(o, nw_ref[...], zh)
    o_ref[...] = _unheads(o, nb, nh)

    @pl.when(c == pl.num_programs(0) - 1)
    def _():
        sa_ref[...] = st_new.reshape(nb, nh, D, D)


def _mixa_prompt(h_a, h_ab, cw, a_log, dt_bias, norm_w, nh):
    nb, t, wa4 = h_a.shape
    C, D = CHUNK, HEAD_DIM
    W = nh * D
    pad = LANES - nh
    al = jnp.pad(a_log, (0, pad)).reshape(1, LANES)
    dtb = jnp.pad(dt_bias, (0, pad)).reshape(1, LANES)
    return pl.pallas_call(
        functools.partial(_mixa_prompt_kernel, nb=nb, nh=nh),
        grid=(t // C,),
        in_specs=[pl.BlockSpec((nb, C, wa4), lambda c: (0, c, 0)),
                  pl.BlockSpec((nb, C, LANES), lambda c: (0, c, 0)),
                  pl.BlockSpec((CONV_W, 3 * W), lambda c: (0, 0)),
                  pl.BlockSpec((1, LANES), lambda c: (0, 0)),
                  pl.BlockSpec((1, LANES), lambda c: (0, 0)),
                  pl.BlockSpec((1, D), lambda c: (0, 0))],
        out_specs=[pl.BlockSpec((nb, C, W), lambda c: (0, c, 0)),
                   pl.BlockSpec((nb, nh, D, D), lambda c: (0, 0, 0, 0))],
        out_shape=[jax.ShapeDtypeStruct((nb, t, W), F32),
                   jax.ShapeDtypeStruct((nb, nh, D, D), F32)],
        scratch_shapes=[pltpu.VMEM((nb * nh, D, D), F32),
                        pltpu.VMEM((nb, 8, 3 * W), F32)],
        compiler_params=_cparams(("arbitrary",)),
        name="mixa_prompt",
    )(h_a, h_ab, cw, al, dtb, norm_w.reshape(1, D))


def _hgrn_prompt_kernel(h_ref, loglb_ref, log1mlb_ref, omlb_ref, nw_ref,
                        o_ref, sc_ref, st_scr, *, nb, nh):
    c = pl.program_id(0)
    C, D = CHUNK, HEAD_DIM
    W = nh * D
    G = nb * nh

    @pl.when(c == 0)
    def _():
        st_scr[...] = jnp.zeros_like(st_scr)

    x = h_ref[...]
    qx = _heads(x[:, :, 0:W], nh)
    fx = _heads(x[:, :, W:2 * W], nh)
    iv = _heads(x[:, :, 2 * W:3 * W], nh)
    zh = _heads(x[:, :, 3 * W:4 * W], nh)

    la = loglb_ref[...]
    lbb = log1mlb_ref[...] - _softplus(-fx)
    mx = jnp.maximum(la, lbb)
    logf = mx + jnp.log1p(jnp.exp(-jnp.abs(la - lbb)))
    kc = omlb_ref[...] * _sigmoid(-fx)
    q = _silu(qx)

    ii = lax.broadcasted_iota(jnp.int32, (C, C), 0)
    jj = lax.broadcasted_iota(jnp.int32, (C, C), 1)
    ri = lax.broadcasted_iota(jnp.int32, (C, D), 0)
    gam = _es_mask('gij,gjk->gik', jnp.broadcast_to((ii >= jj).astype(F32), (G, C, C)), logf)

    def block_starts(hb):
        g4 = gam.reshape(G, C // hb, hb, D)
        starts = g4[:, :, 0:1, :]
        nxt = jnp.concatenate([starts[:, 1:], starts[:, -1:]], axis=1)
        own = jnp.broadcast_to(starts, g4.shape).reshape(G, C, D)
        nx = jnp.broadcast_to(nxt, g4.shape).reshape(G, C, D)
        return own, nx

    attn = jnp.zeros((G, C, C), F32)
    hb = C // 2
    while hb >= 8:
        s = 2 * hb
        own, nx = block_starts(hb)
        upper = (ri % s) >= hb
        qs = jnp.where(upper, q * jnp.exp(gam - own), 0.0)
        ks = jnp.where(upper, 0.0, kc * jnp.exp(jnp.where(upper, 0.0, nx - gam)))
        attn = attn + jnp.where(ii // s == jj // s, _es('gik,gjk->gij', qs, ks), 0.0)
        hb //= 2
    own, _ = block_starts(8)
    qd = q * jnp.exp(gam - own)
    kd = kc * jnp.exp(own - gam)
    attn = attn + jnp.where((ii // 8 == jj // 8) & (ii >= jj), _es('gik,gjk->gij', qd, kd), 0.0)

    st = st_scr[...]
    o = _es('gck,gvk->gcv', q * jnp.exp(gam), st) + _es('gij,gjv->giv', attn, iv)
    glast = gam[:, C - 1:C, :]
    kdec = kc * jnp.exp(glast - gam)
    st_new = st * jnp.exp(glast) + _es('gcv,gck->gvk', iv, kdec)
    st_scr[...] = st_new

    o = _gated_rms(o, nw_ref[...], zh)
    o_ref[...] = _unheads(o, nb, nh)

    @pl.when(c == pl.num_programs(0) - 1)
    def _():
        di = lax.broadcasted_iota(jnp.int32, (D, D), 0)
        dj = lax.broadcasted_iota(jnp.int32, (D, D), 1)
        eye = jnp.broadcast_to((di == dj).astype(F32), (G, D, D))
        sc_ref[...] = _es_mask('gkj,gvj->gkv', eye, st_new).reshape(nb, nh, D, D)


def _hgrn_prompt(h_c, lb, norm_w, nh):
    nb, t, wc4 = h_c.shape
    C, D = CHUNK, HEAD_DIM
    W = nh * D
    G = nb * nh

    def per_group(a):
        return jnp.tile(a.reshape(nh, 1, D), (nb, 1, 1))

    return pl.pallas_call(
        functools.partial(_hgrn_prompt_kernel, nb=nb, nh=nh),
        grid=(t // C,),
        in_specs=[pl.BlockSpec((nb, C, wc4), lambda c: (0, c, 0)),
                  pl.BlockSpec((G, 1, D), lambda c: (0, 0, 0)),
                  pl.BlockSpec((G, 1, D), lambda c: (0, 0, 0)),
                  pl.BlockSpec((G, 1, D), lambda c: (0, 0, 0)),
                  pl.BlockSpec((1, D), lambda c: (0, 0))],
        out_specs=[pl.BlockSpec((nb, C, W), lambda c: (0, c, 0)),
                   pl.BlockSpec((nb, nh, D, D), lambda c: (0, 0, 0, 0))],
        out_shape=[jax.ShapeDtypeStruct((nb, t, W), F32),
                   jax.ShapeDtypeStruct((nb, nh, D, D), F32)],
        scratch_shapes=[pltpu.VMEM((G, D, D), F32)],
        compiler_params=_cparams(("arbitrary",)),
        name="hgrn_prompt",
    )(h_c, per_group(jnp.log(lb)), per_group(jnp.log1p(-lb)), per_group(1.0 - lb),
      norm_w.reshape(1, D))


def _mixa_sample_kernel(al_ref, dtb_ref, xq_ref, xk_ref, xv_ref, z_ref, ab_ref,
                        cwq_ref, cwk_ref, cwv_ref, nw_ref, s0_ref,
                        o_ref, s_ref, kq_scr, *, nh, ts):
    h = pl.program_id(0)
    D = HEAD_DIM
    s_ref[...] = s0_ref[...]
    neg_a = -jnp.exp(al_ref[h])
    dtb = dtb_ref[h]

    def conv(x_ref, cw_ref, t):
        acc = x_ref[t] * cw_ref[0]
        for j in range(1, CONV_W):
            acc = acc + x_ref[t + j] * cw_ref[j]
        return _silu(acc)

    for t in range(ts):
        q = conv(xq_ref, cwq_ref, t)
        k = conv(xk_ref, cwk_ref, t)
        v = conv(xv_ref, cwv_ref, t)
        q = q * lax.rsqrt(jnp.sum(q * q, 0, keepdims=True) + 1e-6) * (D ** -0.5)
        k = k * lax.rsqrt(jnp.sum(k * k, 0, keepdims=True) + 1e-6)
        a = ab_ref[t, pl.ds(h, 1), :]
        b = ab_ref[t, pl.ds(nh + h, 1), :]
        decay = jnp.exp(neg_a * _softplus(a + dtb))
        beta = _sigmoid(b)
        kq_scr[0] = k
        kq_scr[1] = q

        def pass1(d, r):
            kd = kq_scr[0, pl.ds(d, 1), :]
            return r + (s_ref[0, d] * decay) * kd

        r = lax.fori_loop(0, D, pass1, jnp.zeros_like(v))
        vn = beta * (v - r)

        def pass2(d, o):
            kd = kq_scr[0, pl.ds(d, 1), :]
            qd = kq_scr[1, pl.ds(d, 1), :]
            sd = s_ref[0, d] * decay + kd * vn
            s_ref[0, d] = sd
            return o + sd * qd

        o = lax.fori_loop(0, D, pass2, jnp.zeros_like(v))
        o = o * lax.rsqrt(jnp.mean(o * o, 0, keepdims=True) + RMS_EPS)
        o_ref[t] = o * nw_ref[...] * _silu(z_ref[t])


def _mixa_sample(x_t, z_t, ab_t, cw_t, a_log, dt_bias, norm_w, s0_t, nh, ts):
    D = HEAD_DIM
    W = nh * D
    nb = x_t.shape[-1]
    tt = x_t.shape[0]
    nab = ab_t.shape[1]
    smem = pl.BlockSpec(memory_space=pltpu.SMEM)
    return pl.pallas_call(
        functools.partial(_mixa_sample_kernel, nh=nh, ts=ts),
        grid=(nh,),
        in_specs=[smem, smem,
                  pl.BlockSpec((tt, D, nb), lambda h: (0, h, 0)),
                  pl.BlockSpec((tt, D, nb), lambda h: (0, nh + h, 0)),
                  pl.BlockSpec((tt, D, nb), lambda h: (0, 2 * nh + h, 0)),
                  pl.BlockSpec((ts, D, nb), lambda h: (0, h, 0)),
                  pl.BlockSpec((ts, nab, nb), lambda h: (0, 0, 0)),
                  pl.BlockSpec((CONV_W, D, 1), lambda h: (0, h, 0)),
                  pl.BlockSpec((CONV_W, D, 1), lambda h: (0, nh + h, 0)),
                  pl.BlockSpec((CONV_W, D, 1), lambda h: (0, 2 * nh + h, 0)),
                  pl.BlockSpec((D, 1), lambda h: (0, 0)),
                  pl.BlockSpec((1, D, D, nb), lambda h: (h, 0, 0, 0))],
        out_specs=[pl.BlockSpec((ts, D, nb), lambda h: (0, h, 0)),
                   pl.BlockSpec((1, D, D, nb), lambda h: (h, 0, 0, 0))],
        out_shape=[jax.ShapeDtypeStruct((ts, W, nb), F32),
                   jax.ShapeDtypeStruct((nh, D, D, nb), F32)],
        scratch_shapes=[pltpu.VMEM((2, D, nb), F32)],
        compiler_params=_cparams(("arbitrary",)),
        name="mixa_sample",
    )(a_log, dt_bias, x_t, x_t, x_t, z_t, ab_t, cw_t, cw_t, cw_t,
      norm_w.reshape(D, 1), s0_t)


def _hgrn_sample_kernel(q_ref, f_ref, i_ref, z_ref, lb_ref, nw_ref, s0_ref,
                        o_ref, s_ref, kq_scr, *, ts):
    D = HEAD_DIM
    s_ref[...] = s0_ref[...]
    lb = lb_ref[...]
    for t in range(ts):
        fx = f_ref[t]
        f = lb + (1.0 - lb) * _sigmoid(fx)
        kq_scr[0] = f
        kq_scr[1] = (1.0 - lb) * _sigmoid(-fx)
        kq_scr[2] = _silu(q_ref[t])
        v = i_ref[t]

        def body(d, o):
            fd = kq_scr[0, pl.ds(d, 1), :]
            kd = kq_scr[1, pl.ds(d, 1), :]
            qd = kq_scr[2, pl.ds(d, 1), :]
            sd = s_ref[0, d] * fd + kd * v
            s_ref[0, d] = sd
            return o + sd * qd

        o = lax.fori_loop(0, D, body, jnp.zeros_like(v))
        o = o * lax.rsqrt(jnp.mean(o * o, 0, keepdims=True) + RMS_EPS)
        o_ref[t] = o * nw_ref[...] * _silu(z_ref[t])


def _hgrn_sample(hc_t, lb, norm_w, s0_t, nh, ts):
    D = HEAD_DIM
    W = nh * D
    nb = hc_t.shape[-1]
    return pl.pallas_call(
        functools.partial(_hgrn_sample_kernel, ts=ts),
        grid=(nh,),
        in_specs=[pl.BlockSpec((ts, D, nb), lambda h: (0, h, 0)),
                  pl.BlockSpec((ts, D, nb), lambda h: (0, nh + h, 0)),
                  pl.BlockSpec((ts, D, nb), lambda h: (0, 2 * nh + h, 0)),
                  pl.BlockSpec((ts, D, nb), lambda h: (0, 3 * nh + h, 0)),
                  pl.BlockSpec((D, 1), lambda h: (h, 0)),
                  pl.BlockSpec((D, 1), lambda h: (0, 0)),
                  pl.BlockSpec((1, D, D, nb), lambda h: (h, 0, 0, 0))],
        out_specs=[pl.BlockSpec((ts, D, nb), lambda h: (0, h, 0)),
                   pl.BlockSpec((1, D, D, nb), lambda h: (h, 0, 0, 0))],
        out_shape=[jax.ShapeDtypeStruct((ts, W, nb), F32),
                   jax.ShapeDtypeStruct((nh, D, D, nb), F32)],
        scratch_shapes=[pltpu.VMEM((3, D, nb), F32)],
        compiler_params=_cparams(("arbitrary",)),
        name="hgrn_sample",
    )(hc_t, hc_t, hc_t, hc_t, lb.reshape(W, 1), norm_w.reshape(D, 1), s0_t)


def _block_mean_kernel(k_ref, o_ref):
    o_ref[...] = jnp.mean(k_ref[...], axis=0, keepdims=True)


def _block_mean(k):
    nb, t, w = k.shape
    nblk = t // MOBA_BLOCK
    out = pl.pallas_call(
        _block_mean_kernel,
        grid=(nb, nblk),
        in_specs=[pl.BlockSpec((None, MOBA_BLOCK, w), lambda b, n: (b, n, 0))],
        out_specs=pl.BlockSpec((None, None, 1, w), lambda b, n: (b, n, 0, 0)),
        out_shape=jax.ShapeDtypeStruct((nb, nblk, 1, w), F32),
        compiler_params=_cparams(("parallel", "parallel")),
        name="moba_block_mean",
    )(k)
    return out.reshape(nb, nblk, w)


def _alibi_slope(h, nh):
    return float(2.0 ** (-8.0 * (h + 1) / nh))


LOG2E = 1.4426950408889634


def _moba_prompt_kernel(qi_ref, kj_ref, q_ref, k_ref, v_ref, km_ref, o_ref,
                        acc_scr, m_scr, l_scr, sel_scr, qs_scr, bias_scr, *, nh, nblk):
    step = pl.program_id(1)
    qi = qi_ref[step]
    kj = kj_ref[step]
    BS, D = MOBA_BLOCK, HEAD_DIM
    kpos = lax.broadcasted_iota(jnp.int32, (BS, BS), 0)
    qpos = lax.broadcasted_iota(jnp.int32, (BS, BS), 1)

    @pl.when(step == 0)
    def _():
        rel0 = (qpos - kpos).astype(F32)
        for h in range(nh):
            bias_scr[h] = (_alibi_slope(h, nh) * LOG2E) * rel0

    @pl.when(kj == 0)
    def _():
        m_scr[...] = jnp.full_like(m_scr, -jnp.inf)
        l_scr[...] = jnp.zeros_like(l_scr)
        acc_scr[...] = jnp.zeros_like(acc_scr)
        nn = lax.broadcasted_iota(jnp.int32, (nblk, BS), 0)
        elig = nn < qi
        for h in range(nh):
            qh = q_ref[:, h * D:(h + 1) * D]
            qs_scr[h] = (qh * (D ** -0.5 * LOG2E)).astype(BF16)
            gate = _es('nd,qd->nq', km_ref[:, h * D:(h + 1) * D], qh)
            gate = jnp.where(elig, gate, -jnp.inf)
            rank = jnp.zeros((nblk, BS), F32)
            for m in range(nblk):
                gm = gate[m:m + 1, :]
                ahead = (gm > gate) | ((gm == gate) & (nn > m))
                rank = rank + jnp.where(ahead, 1.0, 0.0)
            sel_scr[h] = jnp.where(elig & (rank < MOBA_TOPK), 0.0, -jnp.inf)

    def attend(mask_fn):
        hs = range(nh)
        raw = [jnp.einsum('kd,qd->kq', k_ref[:, h * D:(h + 1) * D].astype(BF16), qs_scr[h],
                          preferred_element_type=F32) for h in hs]
        st = [mask_fn(h, raw[h] - bias_scr[h]) for h in hs]
        m_old = m_scr[...]
        l_old = l_scr[...]
        m_new = jnp.concatenate([jnp.max(st[h], axis=0, keepdims=True) for h in hs], axis=0)
        m_new = jnp.maximum(m_old, m_new)
        m_safe = jnp.where(m_new == -jnp.inf, 0.0, m_new)
        alpha = jnp.exp2(m_old - m_safe)
        p = [jnp.exp2(st[h] - m_safe[h:h + 1, :]) for h in hs]
        l_scr[...] = alpha * l_old + jnp.concatenate(
            [jnp.sum(p[h], axis=0, keepdims=True) for h in hs], axis=0)
        m_scr[...] = m_new
        pv = [jnp.einsum('kd,kq->dq', v_ref[:, h * D:(h + 1) * D].astype(BF16), p[h].astype(BF16),
                         preferred_element_type=F32) for h in hs]
        for h in hs:
            acc_scr[h] = alpha[h:h + 1, :] * acc_scr[h] + pv[h]

    @pl.when(kj < qi)
    def _():
        off = ((qi - kj) * BS).astype(F32)
        attend(lambda h, s: s + (sel_scr[h, pl.ds(kj, 1), :] - (_alibi_slope(h, nh) * LOG2E) * off))

    @pl.when(kj == qi)
    def _():
        attend(lambda h, s: jnp.where(qpos >= kpos, s, -jnp.inf))
        outs = [acc_scr[h] / l_scr[h:h + 1, :] for h in range(nh)]
        o_ref[...] = jnp.concatenate(outs, axis=0).T


def _moba_prompt(q, k, v, nh):
    nb, t, w = q.shape
    BS, D = MOBA_BLOCK, HEAD_DIM
    nblk = t // BS
    km = _block_mean(k)
    pairs = [(i, j) for i in range(nblk) for j in range(i + 1)]
    qi = jnp.asarray([p[0] for p in pairs], jnp.int32)
    kj = jnp.asarray([p[1] for p in pairs], jnp.int32)
    grid_spec = pltpu.PrefetchScalarGridSpec(
        num_scalar_prefetch=2,
        grid=(nb, len(pairs)),
        in_specs=[pl.BlockSpec((None, BS, w), lambda b, s, qi, kj: (b, qi[s], 0)),
                  pl.BlockSpec((None, BS, w), lambda b, s, qi, kj: (b, kj[s], 0)),
                  pl.BlockSpec((None, BS, w), lambda b, s, qi, kj: (b, kj[s], 0)),
                  pl.BlockSpec((None, nblk, w), lambda b, s, qi, kj: (b, 0, 0))],
        out_specs=pl.BlockSpec((None, BS, w), lambda b, s, qi, kj: (b, qi[s], 0)),
        scratch_shapes=[pltpu.VMEM((nh, D, BS), F32),
                        pltpu.VMEM((nh, BS), F32),
                        pltpu.VMEM((nh, BS), F32),
                        pltpu.VMEM((nh, nblk, BS), F32),
                        pltpu.VMEM((nh, BS, D), BF16),
                        pltpu.VMEM((nh, BS, BS), F32)])
    return pl.pallas_call(
        functools.partial(_moba_prompt_kernel, nh=nh, nblk=nblk),
        grid_spec=grid_spec,
        out_shape=jax.ShapeDtypeStruct((nb, t, w), F32),
        compiler_params=_cparams(("parallel", "arbitrary")),
        name="moba_prompt",
    )(qi, kj, q, k, v, km)


def _moba_sample_kernel(pt_ref, q_ref, kn_ref, vn_ref, *refs, nh, ts, past_len, bps):
    npg = 2 * bps
    k_refs, v_refs = refs[:npg], refs[npg:2 * npg]
    o_ref, m_scr, l_scr, g_scr, acc_scr = refs[2 * npg:]
    step = pl.program_id(1)
    BS, D = MOBA_BLOCK, HEAD_DIM
    page = BS // 2
    TP = q_ref.shape[1]
    scale = D ** -0.5
    q = q_ref[...]
    qb = q.astype(BF16)

    tpos = lax.broadcasted_iota(jnp.int32, (nh, TP, 1), 1).astype(F32)
    hidx = lax.broadcasted_iota(jnp.int32, (nh, TP, 1), 0)
    slope = jnp.zeros((nh, TP, 1), F32)
    for h in range(nh):
        slope = jnp.where(hidx == h, _alibi_slope(h, nh), slope)
    kofs = lax.broadcasted_iota(jnp.int32, (nh, TP, page), 2).astype(F32)
    qpos = past_len + tpos

    pages = range(npg)
    raw = [jnp.einsum('htd,hdk->htk', qb, k_refs[i][...].astype(BF16), preferred_element_type=F32)
           for i in pages]
    scores = []
    for i in pages:
        kpos = ((step * bps + i // 2) * BS + (i % 2) * page).astype(F32) + kofs
        scores.append(raw[i] * scale - slope * (qpos - kpos))
    m_blk = [jnp.maximum(jnp.max(scores[2 * b], axis=-1, keepdims=True),
                         jnp.max(scores[2 * b + 1], axis=-1, keepdims=True)) for b in range(bps)]
    p = [jnp.exp(scores[i] - m_blk[i // 2]) for i in pages]
    pv = [jnp.einsum('htk,hdk->htd', p[i].astype(BF16), v_refs[i][...].astype(BF16),
                     preferred_element_type=F32) for i in pages]
    for b in range(bps):
        n = step * bps + b
        l_n = (jnp.sum(p[2 * b], axis=-1, keepdims=True)
               + jnp.sum(p[2 * b + 1], axis=-1, keepdims=True))
        gate = (jnp.sum(raw[2 * b], axis=-1, keepdims=True)
                + jnp.sum(raw[2 * b + 1], axis=-1, keepdims=True)) * (1.0 / BS)
        m_scr[n] = jnp.broadcast_to(m_blk[b], (nh, TP, LANES))
        l_scr[n] = jnp.broadcast_to(l_n, (nh, TP, LANES))
        g_scr[n] = jnp.broadcast_to(gate, (nh, TP, LANES))
        acc_scr[n] = pv[2 * b] + pv[2 * b + 1]

    @pl.when(step == pl.num_programs(1) - 1)
    def _():
        nb_past = past_len // BS
        gates = [g_scr[i][:, :, 0:1] for i in range(nb_past)]
        sels = []
        for i in range(nb_past):
            rank = jnp.zeros((nh, TP, 1), F32)
            for m in range(nb_past):
                if m == i:
                    continue
                ahead = (gates[m] > gates[i]) if m > i else (gates[m] >= gates[i])
                rank = rank + jnp.where(ahead, 1.0, 0.0)
            sels.append(rank < MOBA_TOPK)
        kn = kn_ref[...]
        vn = vn_ref[...]
        s_own = []
        for j in range(ts):
            sj = jnp.sum(q * kn[:, j:j + 1, :], axis=-1, keepdims=True) * scale
            sj = sj - slope * (tpos - j)
            s_own.append(jnp.where(tpos >= j, sj, -jnp.inf))
        m_tot = s_own[0]
        for j in range(1, ts):
            m_tot = jnp.maximum(m_tot, s_own[j])
        for i in range(nb_past):
            m_tot = jnp.maximum(m_tot, jnp.where(sels[i], m_scr[i][:, :, 0:1], -jnp.inf))
        num = jnp.zeros((nh, TP, D), F32)
        den = jnp.zeros((nh, TP, 1), F32)
        for j in range(ts):
            pj = jnp.exp(s_own[j] - m_tot)
            num = num + pj * vn[:, j:j + 1, :]
            den = den + pj
        for i in range(nb_past):
            wi = jnp.where(sels[i], jnp.exp(m_scr[i][:, :, 0:1] - m_tot), 0.0)
            num = num + wi * acc_scr[i]
            den = den + wi * l_scr[i][:, :, 0:1]
        o_ref[...] = num / den


def _moba_sample(q, kn, vn, cache_kt, cache_vt, page_table, layer, ts):
    nb, nh, TP, D = q.shape
    page = cache_kt.shape[4]
    n_pages = page_table.shape[1]
    past_len = n_pages * page
    BS = MOBA_BLOCK
    assert BS == 2 * page and past_len % BS == 0 and ts <= TP
    nblk = past_len // BS
    bps = next(c for c in (4, 2, 1) if nblk % c == 0)
    npg = 2 * bps
    pt = page_table.reshape(-1)

    def page_spec(i):
        return pl.BlockSpec((None, None, nh, D, page),
                            lambda b, s, pt: (layer, pt[b * n_pages + s * npg + i], 0, 0, 0))

    tok = pl.BlockSpec((None, nh, TP, D), lambda b, s, pt: (b, 0, 0, 0))
    pages = [page_spec(i) for i in range(npg)]
    grid_spec = pltpu.PrefetchScalarGridSpec(
        num_scalar_prefetch=1,
        grid=(nb, nblk // bps),
        in_specs=[tok, tok, tok] + pages + pages,
        out_specs=tok,
        scratch_shapes=[pltpu.VMEM((nblk, nh, TP, LANES), F32),
                        pltpu.VMEM((nblk, nh, TP, LANES), F32),
                        pltpu.VMEM((nblk, nh, TP, LANES), F32),
                        pltpu.VMEM((nblk, nh, TP, D), F32)])
    return pl.pallas_call(
        functools.partial(_moba_sample_kernel, nh=nh, ts=ts, past_len=past_len, bps=bps),
        grid_spec=grid_spec,
        out_shape=jax.ShapeDtypeStruct((nb, nh, TP, D), F32),
        compiler_params=_cparams(("parallel", "arbitrary")),
        name="moba_sample",
    )(pt, q, kn, vn, *([cache_kt] * npg), *([cache_vt] * npg))


def _router_gates(y, r, n_exp):
    logits = _es('mk,kn->mn', y, r)
    lane = lax.broadcasted_iota(jnp.int32, logits.shape, 1)
    logits = jnp.where(lane < n_exp, logits, -jnp.inf)
    m1 = jnp.max(logits, -1, keepdims=True)
    i1 = jnp.min(jnp.where(logits == m1, lane, LANES), -1, keepdims=True)
    first = lane == i1
    rest = jnp.where(first, -jnp.inf, logits)
    m2 = jnp.max(rest, -1, keepdims=True)
    i2 = jnp.min(jnp.where(rest == m2, lane, LANES), -1, keepdims=True)
    second = lane == i2
    e = jnp.exp(m2 - m1)
    den = 1.0 + e
    return jnp.where(first, 1.0 / den, 0.0) + jnp.where(second, e / den, 0.0)


def _out_proj_kernel(*refs, alpha, wa, wb, n_exp):
    if n_exp:
        x_ref, oa_ref, ob_ref, oc_ref, w_ref, g_ref, b_ref, r_ref, o_ref, gate_ref = refs
    else:
        x_ref, oa_ref, ob_ref, oc_ref, w_ref, g_ref, b_ref, o_ref = refs
    acc = jnp.dot(oa_ref[...].astype(BF16), w_ref[0:wa, :], preferred_element_type=F32)
    acc = acc + jnp.dot(ob_ref[...].astype(BF16), w_ref[wa:wa + wb, :], preferred_element_type=F32)
    acc = acc + jnp.dot(oc_ref[...].astype(BF16), w_ref[wa + wb:, :], preferred_element_type=F32)
    y = _layer_norm(alpha * x_ref[...] + acc, g_ref[...], b_ref[...])
    o_ref[...] = y
    if n_exp:
        gate_ref[...] = _router_gates(y, r_ref[...], n_exp)


def _out_proj(x, oa, ob, oc, w, g, b, router, alpha, tm):
    n, d = x.shape
    wa, wb, wc = oa.shape[1], ob.shape[1], oc.shape[1]
    n_exp = 0 if router is None else router.shape[1]
    row = lambda width: pl.BlockSpec((tm, width), lambda i: (i, 0))
    full = lambda shape: pl.BlockSpec(shape, lambda i: (0, 0))
    in_specs = [row(d), row(wa), row(wb), row(wc), full(w.shape), full((1, d)), full((1, d))]
    args = [x, oa, ob, oc, w, g.reshape(1, d), b.reshape(1, d)]
    out_specs = [row(d)]
    out_shape = [jax.ShapeDtypeStruct((n, d), F32)]
    if n_exp:
        in_specs.append(full((d, LANES)))
        args.append(jnp.pad(router, ((0, 0), (0, LANES - n_exp))))
        out_specs.append(row(LANES))
        out_shape.append(jax.ShapeDtypeStruct((n, LANES), F32))
    res = pl.pallas_call(
        functools.partial(_out_proj_kernel, alpha=alpha, wa=wa, wb=wb, n_exp=n_exp),
        grid=(n // tm,),
        in_specs=in_specs, out_specs=out_specs, out_shape=out_shape,
        compiler_params=_cparams(("parallel",)),
        name="out_proj_ln",
    )(*args)
    return (res[0], res[1]) if n_exp else (res[0], None)


def _ffn_kernel(x_ref, w1_ref, w3_ref, w2_ref, g_ref, b_ref, o_ref, acc_scr, xb_scr, *, alpha):
    f = pl.program_id(1)

    @pl.when(f == 0)
    def _():
        acc_scr[...] = jnp.zeros_like(acc_scr)
        xb_scr[...] = x_ref[...].astype(BF16)

    xb = xb_scr[...]
    h1 = jnp.dot(xb, w1_ref[...].astype(BF16), preferred_element_type=F32)
    h3 = jnp.dot(xb, w3_ref[...].astype(BF16), preferred_element_type=F32)
    hid = (_silu(h1) * h3).astype(BF16)
    acc_scr[...] += jnp.dot(hid, w2_ref[...].astype(BF16), preferred_element_type=F32)

    @pl.when(f == pl.num_programs(1) - 1)
    def _():
        o_ref[...] = _layer_norm(alpha * x_ref[...] + acc_scr[...], g_ref[...], b_ref[...])


def _ffn_dense(x, w1, w3, w2, g, b, alpha, tm, tf):
    n, d = x.shape
    dff = w1.shape[1]
    return pl.pallas_call(
        functools.partial(_ffn_kernel, alpha=alpha),
        grid=(n // tm, dff // tf),
        in_specs=[pl.BlockSpec((tm, d), lambda i, f: (i, 0)),
                  pl.BlockSpec((d, tf), lambda i, f: (0, f)),
                  pl.BlockSpec((d, tf), lambda i, f: (0, f)),
                  pl.BlockSpec((tf, d), lambda i, f: (f, 0)),
                  pl.BlockSpec((1, d), lambda i, f: (0, 0)),
                  pl.BlockSpec((1, d), lambda i, f: (0, 0))],
        out_specs=pl.BlockSpec((tm, d), lambda i, f: (i, 0)),
        out_shape=jax.ShapeDtypeStruct((n, d), F32),
        scratch_shapes=[pltpu.VMEM((tm, d), F32), pltpu.VMEM((tm, d), BF16)],
        compiler_params=_cparams(("parallel", "arbitrary")),
        name="ffn_dense",
    )(x, w1, w3, w2, g.reshape(1, d), b.reshape(1, d))


def _moe_kernel(x_ref, gate_ref, w1_ref, w3_ref, w2_ref, g_ref, b_ref, o_ref,
                acc_scr, xb_scr, xc_scr, yc_scr, posc_scr, post_scr, selt_scr, cnt_smem,
                *, alpha, ch, nch):
    e = pl.program_id(1)
    f = pl.program_id(2)
    first_f = f == 0
    last_f = f == pl.num_programs(2) - 1
    tm = x_ref.shape[0]

    @pl.when((e == 0) & first_f)
    def _():
        acc_scr[...] = jnp.zeros_like(acc_scr)
        xb_scr[...] = x_ref[...].astype(BF16)
        sel = jnp.where(gate_ref[...] > 0.0, 1.0, 0.0)
        ri = lax.broadcasted_iota(jnp.int32, (tm, tm), 0)
        ci = lax.broadcasted_iota(jnp.int32, (tm, tm), 1)
        before = jnp.where(ci < ri, 1.0, 0.0).astype(BF16)
        pos = jnp.dot(before, sel.astype(BF16), preferred_element_type=F32)
        posc_scr[...] = pos
        post_scr[...] = pos.T
        selt_scr[...] = sel.T

    @pl.when(first_f)
    def _():
        lane = lax.broadcasted_iota(jnp.int32, (tm, LANES), 1)
        cnt = jnp.sum(jnp.where((lane == e) & (gate_ref[...] > 0.0), 1.0, 0.0))
        cnt_smem[0] = cnt.astype(jnp.int32)
        prow = post_scr[pl.ds(e, 1), :]
        srow = selt_scr[pl.ds(e, 1), :]
        for k in range(nch):
            @pl.when(cnt_smem[0] > k * ch)
            def _():
                r = (lax.broadcasted_iota(jnp.int32, (ch, tm), 0) + k * ch).astype(F32)
                pk = jnp.where((prow == r) & (srow > 0.0), 1.0, 0.0).astype(BF16)
                xc_scr[k] = jnp.dot(pk, xb_scr[...], preferred_element_type=F32).astype(BF16)

    cnt = cnt_smem[0]
    for k in range(nch):
        @pl.when(cnt > k * ch)
        def _():
            xk = xc_scr[k]
            h1 = jnp.dot(xk, w1_ref[...].astype(BF16), preferred_element_type=F32)
            h3 = jnp.dot(xk, w3_ref[...].astype(BF16), preferred_element_type=F32)
            hid = (_silu(h1) * h3).astype(BF16)
            part = jnp.dot(hid, w2_ref[...].astype(BF16), preferred_element_type=F32)

            @pl.when(first_f)
            def _():
                yc_scr[k] = part

            @pl.when(jnp.logical_not(first_f))
            def _():
                yc_scr[k] += part

    @pl.when(last_f)
    def _():
        lane = lax.broadcasted_iota(jnp.int32, (tm, LANES), 1)
        mine = lane == e
        gcol = jnp.sum(jnp.where(mine, gate_ref[...], 0.0), -1, keepdims=True)
        pcol = jnp.sum(jnp.where(mine, posc_scr[...], 0.0), -1, keepdims=True)
        for k in range(nch):
            @pl.when(cnt > k * ch)
            def _():
                r = (lax.broadcasted_iota(jnp.int32, (tm, ch), 1) + k * ch).astype(F32)
                ptk = jnp.where((pcol == r) & (gcol > 0.0), 1.0, 0.0).astype(BF16)
                acc_scr[...] += gcol * jnp.dot(ptk, yc_scr[k].astype(BF16),
                                               preferred_element_type=F32)

    @pl.when((e == pl.num_programs(1) - 1) & last_f)
    def _():
        o_ref[...] = _layer_norm(alpha * x_ref[...] + acc_scr[...], g_ref[...], b_ref[...])


def _moe_weight_tiles(w1, w3, w2, tf):
    n_exp, d, dff = w1.shape

    def tiles(w):
        return w.astype(BF16).reshape(n_exp, d, dff // tf, tf).transpose(0, 2, 1, 3)

    return tiles(w1), tiles(w3), w2.astype(BF16)


def _ffn_moe(x, gate, w1, w3, w2, g, b, alpha, tm):
    n, d = x.shape
    n_exp, nf, _, tf = w1.shape
    dff = nf * tf
    ch = min(tm, -(-(tm * MOE_TOPK * 3) // (n_exp * 2 * LANES)) * LANES)
    nch = -(-tm // ch)
    return pl.pallas_call(
        functools.partial(_moe_kernel, alpha=alpha, ch=ch, nch=nch),
        grid=(n // tm, n_exp, dff // tf),
        in_specs=[pl.BlockSpec((tm, d), lambda i, e, f: (i, 0), pipeline_mode=pl.Buffered(1)),
                  pl.BlockSpec((tm, LANES), lambda i, e, f: (i, 0)),
                  pl.BlockSpec((None, None, d, tf), lambda i, e, f: (e, f, 0, 0)),
                  pl.BlockSpec((None, None, d, tf), lambda i, e, f: (e, f, 0, 0)),
                  pl.BlockSpec((None, tf, d), lambda i, e, f: (e, f, 0)),
                  pl.BlockSpec((1, d), lambda i, e, f: (0, 0)),
                  pl.BlockSpec((1, d), lambda i, e, f: (0, 0))],
        out_specs=pl.BlockSpec((tm, d), lambda i, e, f: (i, 0), pipeline_mode=pl.Buffered(1)),
        out_shape=jax.ShapeDtypeStruct((n, d), F32),
        scratch_shapes=[pltpu.VMEM((tm, d), F32), pltpu.VMEM((tm, d), BF16),
                        pltpu.VMEM((nch, ch, d), BF16), pltpu.VMEM((nch, ch, d), F32),
                        pltpu.VMEM((tm, LANES), F32), pltpu.VMEM((LANES, tm), F32),
                        pltpu.VMEM((LANES, tm), F32), pltpu.SMEM((1,), jnp.int32)],
        compiler_params=_cparams(("parallel", "arbitrary", "arbitrary")),
        name="ffn_moe",
    )(x, gate, w1, w3, w2, g.reshape(1, d), b.reshape(1, d))


def _row_tile(n, cap):
    t = cap
    while n % t:
        t //= 2
    return t


def _ff_tile(dff):
    for t in (512, 256, 128):
        if dff % t == 0:
            return t
    return dff


def kernel(x_prompt, x_sample, cache_k, cache_v, page_table, state_a, state_a_conv, state_c, ln_in_g, ln_in_b, w_in, w_out, a_conv_w, a_a_log, a_dt_bias, a_norm_w, c_lb_logits, c_norm_w, ln1_g, ln1_b, ln2_g, ln2_b, ffn_w1, ffn_w3, ffn_w2, moe_router, moe_w1, moe_w3, moe_w2):
    bp, tp, d = x_prompt.shape
    bs, ts, _ = x_sample.shape
    depth = w_in.shape[0]
    D = HEAD_DIM
    wa, wb = d // 4, d // 2
    wc = d - wa - wb
    ha, hb, hc = wa // D, wb // D, wc // D
    kc = hc * D
    alpha = (2.0 * depth) ** 0.25
    off_b = 4 * wa + 2 * ha
    off_c = off_b + 3 * wb
    p_in = off_c + 2 * kc + 2 * wc
    assert w_in.shape[2] == p_in and 2 * ha <= LANES

    seg_w = [4 * wa, wb, wb, wb, 2 * kc + 2 * wc, LANES]
    segs, lo = [], 0
    for sw in seg_w:
        segs.append((lo, lo + sw))
        lo += sw

    lb_all = jnp.cumsum(jax.nn.softmax(c_lb_logits.astype(F32), axis=0), axis=0)
    lb_all = lb_all - lb_all[0:1]

    n_pool, page = cache_k.shape[1], cache_k.shape[2]
    ck = cache_k.transpose(0, 1, 3, 4, 2)
    cv = cache_v.transpose(0, 1, 3, 4, 2)
    tp_pad = -(-ts // 8) * 8

    np_rows, ns_rows = bp * tp, bs * ts
    tm_p = _row_tile(np_rows, 256)
    tm_s = _row_tile(ns_rows, 256)
    xp = _ln_rows(x_prompt.reshape(np_rows, d), ln_in_g, ln_in_b, tm_p)
    xs = _ln_rows(x_sample.reshape(ns_rows, d), ln_in_g, ln_in_b, tm_s)

    outs = {k: [] for k in ("kp", "vp", "ks", "vs", "sap", "sas", "cvp", "cvs", "scp", "scs")}
    for l in range(depth):
        wl = w_in[l]
        w_perm = jnp.concatenate(
            [wl[:, :4 * wa], wl[:, off_b:], wl[:, 4 * wa:off_b],
             jnp.zeros((d, LANES - 2 * ha), wl.dtype)], axis=1).astype(BF16)
        w_o = w_out[l].astype(BF16)
        cw = a_conv_w[l]

        h_a, q_b, k_b, v_b, h_c, h_ab = _in_proj(xp, w_perm, segs, tm_p)
        h_a3 = h_a.reshape(bp, tp, 4 * wa)
        oa, sa_new = _mixa_prompt(h_a3, h_ab.reshape(bp, tp, LANES), cw, a_a_log[l], a_dt_bias[l],
                                  a_norm_w[l], ha)
        ob = _moba_prompt(q_b.reshape(bp, tp, wb), k_b.reshape(bp, tp, wb), v_b.reshape(bp, tp, wb), hb)
        oc, sc_new = _hgrn_prompt(h_c.reshape(bp, tp, 4 * wc), lb_all[l], c_norm_w[l], hc)
        outs["kp"].append(k_b.reshape(bp, tp, hb, D))
        outs["vp"].append(v_b.reshape(bp, tp, hb, D))
        outs["sap"].append(sa_new)
        conv_full = jnp.concatenate([jnp.zeros((bp, CONV_W - 1, 3 * wa), F32), h_a3[:, :, :3 * wa]], axis=1) \
            if tp < CONV_W - 1 else h_a3[:, :, :3 * wa]
        outs["cvp"].append(conv_full[:, conv_full.shape[1] - (CONV_W - 1):])
        outs["scp"].append(sc_new)
        router = moe_router[l // 2] if l % 2 else None
        x1p, gate_p = _out_proj(xp, oa.reshape(np_rows, wa), ob.reshape(np_rows, wb),
                                oc.reshape(np_rows, wc), w_o, ln1_g[l], ln1_b[l], router, alpha, tm_p)

        h_a, q_b, k_b, v_b, h_c, h_ab = _in_proj(xs, w_perm, segs, tm_s)
        h_a3 = h_a.reshape(bs, ts, 4 * wa)
        x_full = jnp.concatenate([state_a_conv[l].astype(F32), h_a3[:, :, :3 * wa]], axis=1)
        x_t = x_full.transpose(1, 2, 0)
        z_t = h_a3[:, :, 3 * wa:].transpose(1, 2, 0)
        ab_t = h_ab.reshape(bs, ts, LANES)[:, :, :max(8, 2 * ha)].transpose(1, 2, 0)
        s0_t = state_a[l].astype(F32).transpose(1, 2, 3, 0)
        oa_t, sa_t = _mixa_sample(x_t, z_t, ab_t, cw.reshape(CONV_W, 3 * wa, 1), a_a_log[l], a_dt_bias[l],
                                  a_norm_w[l], s0_t, ha, ts)
        oa = oa_t.transpose(2, 0, 1).reshape(ns_rows, wa)
        outs["sas"].append(sa_t.transpose(3, 0, 1, 2))
        outs["cvs"].append(x_full[:, x_full.shape[1] - (CONV_W - 1):])
        def head_major(a):
            a = a.reshape(bs, ts, hb, D).transpose(0, 2, 1, 3)
            return jnp.pad(a, ((0, 0), (0, 0), (0, tp_pad - ts), (0, 0)))

        ob = _moba_sample(head_major(q_b), head_major(k_b), head_major(v_b), ck, cv, page_table, l, ts)
        ob = ob[:, :, :ts].transpose(0, 2, 1, 3)
        outs["ks"].append(k_b.reshape(bs, ts, hb, D))
        outs["vs"].append(v_b.reshape(bs, ts, hb, D))
        hc_t = h_c.reshape(bs, ts, 4 * wc).transpose(1, 2, 0)
        sc0_t = state_c[l].astype(F32).transpose(1, 2, 3, 0)
        oc_t, sc_t = _hgrn_sample(hc_t, lb_all[l], c_norm_w[l], sc0_t, hc, ts)
        oc = oc_t.transpose(2, 0, 1).reshape(ns_rows, wc)
        outs["scs"].append(sc_t.transpose(3, 0, 1, 2))
        x1s, gate_s = _out_proj(xs, oa, ob.reshape(ns_rows, wb), oc, w_o, ln1_g[l], ln1_b[l], router,
                                alpha, tm_s)

        if l % 2 == 0:
            w1, w3, w2 = ffn_w1[l // 2].astype(BF16), ffn_w3[l // 2].astype(BF16), ffn_w2[l // 2].astype(BF16)
            tf = _ff_tile(w1.shape[1])
            xp = _ffn_dense(x1p, w1, w3, w2, ln2_g[l], ln2_b[l], alpha, _row_tile(np_rows, 1024), tf)
            xs = _ffn_dense(x1s, w1, w3, w2, ln2_g[l], ln2_b[l], alpha, _row_tile(ns_rows, 1024), tf)
        else:
            dff = moe_w1.shape[3]
            tf = dff // 2 if dff % (2 * LANES) == 0 else _ff_tile(dff)
            w1, w3, w2 = _moe_weight_tiles(moe_w1[l // 2], moe_w3[l // 2], moe_w2[l // 2], tf)
            xp = _ffn_moe(x1p, gate_p, w1, w3, w2, ln2_g[l], ln2_b[l], alpha, _row_tile(np_rows, 1024))
            xs = _ffn_moe(x1s, gate_s, w1, w3, w2, ln2_g[l], ln2_b[l], alpha, _row_tile(ns_rows, 1024))

    st = lambda key: jnp.stack(outs[key])
    return (xp.reshape(bp, tp, d), xs.reshape(bs, ts, d), st("kp"), st("vp"), st("ks"), st("vs"),
            st("sap"), st("sas"), st("cvp"), st("cvs"), st("scp"), st("scs"))
```

```python
import functools
import math

import numpy as np
import jax
import jax.numpy as jnp
from jax import lax
from jax.experimental import pallas as pl
from jax.experimental.pallas import tpu as pltpu

HEAD_DIM = 64
CONV_W = 4
CHUNK = 64
MOBA_BLOCK = 256
MOBA_TOPK = 3
MOE_TOPK = 2
LN_EPS = 1e-5
RMS_EPS = 1e-6
LANES = 128
VMEM_LIMIT = 56 * 1024 * 1024
HI = lax.Precision.HIGHEST
F32 = jnp.float32
BF16 = jnp.bfloat16


def _cparams(sem):
    return pltpu.CompilerParams(dimension_semantics=sem, vmem_limit_bytes=VMEM_LIMIT)


def _bdot(a, b):
    return jnp.dot(a.astype(BF16), b.astype(BF16), preferred_element_type=F32)


def _split2(a):
    hi = a.astype(BF16)
    return hi, (a - hi.astype(F32)).astype(BF16)


def _es(spec, a, b):
    ah, al = _split2(a)
    bh, bl = _split2(b)
    e = functools.partial(jnp.einsum, spec, preferred_element_type=F32)
    return e(ah, bh) + (e(ah, bl) + e(al, bh))


def _es_mask(spec, m, b):
    bh = b.astype(BF16)
    r = b - bh.astype(F32)
    bm = r.astype(BF16)
    bl = (r - bm.astype(F32)).astype(BF16)
    mb = m.astype(BF16)
    e = functools.partial(jnp.einsum, spec, preferred_element_type=F32)
    return e(mb, bh) + (e(mb, bm) + e(mb, bl))


def _layer_norm(x, g, b):
    mu = jnp.mean(x, -1, keepdims=True)
    xc = x - mu
    var = jnp.mean(xc * xc, -1, keepdims=True)
    return xc * lax.rsqrt(var + LN_EPS) * g + b


def _sigmoid(x):
    return 1.0 / (1.0 + jnp.exp(-x))


def _silu(x):
    return x * _sigmoid(x)


def _softplus(x):
    return jnp.maximum(x, 0.0) + jnp.log1p(jnp.exp(-jnp.abs(x)))


def _ln_kernel(x_ref, g_ref, b_ref, o_ref):
    o_ref[...] = _layer_norm(x_ref[...], g_ref[...], b_ref[...])


def _ln_rows(x, g, b, tm):
    n, d = x.shape
    return pl.pallas_call(
        _ln_kernel,
        grid=(n // tm,),
        in_specs=[pl.BlockSpec((tm, d), lambda i: (i, 0)),
                  pl.BlockSpec((1, d), lambda i: (0, 0)),
                  pl.BlockSpec((1, d), lambda i: (0, 0))],
        out_specs=pl.BlockSpec((tm, d), lambda i: (i, 0)),
        out_shape=jax.ShapeDtypeStruct((n, d), F32),
        compiler_params=_cparams(("parallel",)),
        name="ln_rows",
    )(x, g.reshape(1, d), b.reshape(1, d))


def _in_proj_kernel(x_ref, w_ref, *refs, segs, kv_segs, n_alias):
    refs = refs[n_alias:]
    o_refs = refs[:len(segs)]
    t_refs = refs[len(segs):]
    xb = x_ref[...].astype(BF16)
    for si, (o_ref, (lo, hi)) in enumerate(zip(o_refs, segs)):
        val = jnp.dot(xb, w_ref[:, lo:hi], preferred_element_type=F32)
        o_ref[...] = val
        if si in kv_segs:
            t_ref = t_refs[kv_segs.index(si)]
            t_ref[...] = val.T.reshape(t_ref.shape)


def _in_proj(x, w, segs, tm, kv_stack=None):
    n, d = x.shape
    in_specs = [pl.BlockSpec((tm, d), lambda i: (i, 0)),
                pl.BlockSpec(w.shape, lambda i: (0, 0))]
    out_specs = [pl.BlockSpec((tm, hi - lo), lambda i: (i, 0)) for lo, hi in segs]
    out_shape = [jax.ShapeDtypeStruct((n, hi - lo), F32) for lo, hi in segs]
    args, aliases, kv_segs, n_alias = [x, w], {}, (), 0
    if kv_stack is not None:
        layer, depth, nb, t, nh, kv_segs, prev = kv_stack
        per_b = t // tm
        for j in range(len(kv_segs)):
            out_specs.append(pl.BlockSpec((None, None, nh, HEAD_DIM, tm),
                                          lambda i: (layer, i // per_b, 0, 0, i % per_b)))
            out_shape.append(jax.ShapeDtypeStruct((depth, nb, nh, HEAD_DIM, t), F32))
            if prev is not None:
                in_specs.append(pl.BlockSpec(memory_space=pl.ANY))
                args.append(prev[j])
                aliases[2 + j] = len(segs) + j
        n_alias = len(args) - 2
    return pl.pallas_call(
        functools.partial(_in_proj_kernel, segs=segs, kv_segs=tuple(kv_segs), n_alias=n_alias),
        grid=(n // tm,),
        in_specs=in_specs, out_specs=out_specs, out_shape=out_shape,
        input_output_aliases=aliases,
        compiler_params=_cparams(("parallel",)),
        name="in_proj",
    )(*args)


def _heads(a, nh):
    nb, c, _ = a.shape
    d = HEAD_DIM
    st = jnp.stack([a[:, :, h * d:(h + 1) * d] for h in range(nh)], axis=1)
    return st.reshape(nb * nh, c, d)


def _unheads(o, nb, nh):
    _, c, d = o.shape
    o4 = o.reshape(nb, nh, c, d)
    return jnp.concatenate([o4[:, h] for h in range(nh)], axis=-1)


def _gated_rms(o, w, z):
    o = o * lax.rsqrt(jnp.mean(o * o, -1, keepdims=True) + RMS_EPS)
    return o * w * _silu(z)


def _mixa_prompt_kernel(h_ref, ab_ref, cw_ref, al_ref, dtb_ref, nw_ref,
                        o_ref, sa_ref, s_scr, prev_scr, *, nb, nh):
    c = pl.program_id(0)
    C, D = CHUNK, HEAD_DIM
    W = nh * D
    G = nb * nh

    @pl.when(c == 0)
    def _():
        s_scr[...] = jnp.zeros_like(s_scr)
        prev_scr[...] = jnp.zeros_like(prev_scr)

    x = h_ref[:, :, 0:3 * W]
    z = h_ref[:, :, 3 * W:4 * W]
    xp = jnp.concatenate([prev_scr[...], x], axis=1)
    cw = cw_ref[...]
    y = (xp[:, 5:5 + C] * cw[0] + xp[:, 6:6 + C] * cw[1]
         + xp[:, 7:7 + C] * cw[2] + xp[:, 8:8 + C] * cw[3])
    prev_scr[...] = x[:, C - 8:C]
    y = _silu(y)

    q = _heads(y[:, :, 0:W], nh)
    k = _heads(y[:, :, W:2 * W], nh)
    v = _heads(y[:, :, 2 * W:3 * W], nh)
    zh = _heads(z, nh)
    q = q * lax.rsqrt(jnp.sum(q * q, -1, keepdims=True) + 1e-6) * (D ** -0.5)
    k = k * lax.rsqrt(jnp.sum(k * k, -1, keepdims=True) + 1e-6)

    ab = ab_ref[...]
    dt = _softplus(ab + dtb_ref[...])
    gfull = -jnp.exp(al_ref[...]) * dt
    bfull = _sigmoid(ab)

    def lane_bc(a, off):
        st = jnp.stack([jnp.broadcast_to(a[:, :, off + h:off + h + 1], (nb, C, D))
                        for h in range(nh)], axis=1)
        return st.reshape(G, C, D)

    gb = lane_bc(gfull, 0)
    beta = lane_bc(bfull, nh)

    ii = lax.broadcasted_iota(jnp.int32, (C, C), 0)
    jj = lax.broadcasted_iota(jnp.int32, (C, C), 1)

    def bcg(m):
        return jnp.broadcast_to(m.astype(F32), (G, C, C))

    gam = _es_mask('gij,gjk->gik', bcg(ii >= jj), gb)
    gam_t = jnp.einsum('gik,gjk->gij', bcg(jj == 0), gam, precision=HI,
                       preferred_element_type=F32)
    dec = jnp.exp(jnp.where(ii >= jj, gam - gam_t, -jnp.inf))

    kb = k * beta
    nmat = jnp.where(ii > jj, _es('gik,gjk->gij', kb, k) * dec, 0.0)
    xinv = (ii == jj).astype(F32) - jnp.where(ii // 2 == jj // 2, nmat, 0.0)
    s = 4
    while s <= C:
        off = jnp.where((ii // s == jj // s) & (ii // (s // 2) != jj // (s // 2)), nmat, 0.0)
        xinv = xinv - _es('gij,gjk->gik', xinv, _es('gij,gjk->gik', off, xinv))
        s *= 2

    egam = jnp.exp(gam)
    rhs = jnp.concatenate([v * beta, kb * egam], axis=-1)
    sol = _es('gij,gjk->gik', xinv, rhs)
    st = s_scr[...]
    u = sol[:, :, 0:D] - _es('gck,gkv->gcv', sol[:, :, D:2 * D], st)
    attn = _es('gik,gjk->gij', q, k) * dec
    o = _es('gck,gkv->gcv', q * egam, st) + _es('gij,gjv->giv', attn, u)
    glast = gam[:, C - 1:C, :]
    kdec = k * jnp.exp(glast - gam)
    st_new = st * jnp.exp(glast) + _es('gck,gcv->gkv', kdec, u)
    s_scr[...] = st_new

    o = _gated_rms(o, nw_ref[...], zh)
    o_ref[...] = _unheads(o, nb, nh)

    @pl.when(c == pl.num_programs(0) - 1)
    def _():
        sa_ref[...] = st_new.reshape(nb, nh, D, D)


def _mixa_prompt(h_a, h_ab, cw, a_log, dt_bias, norm_w, nh):
    nb, t, wa4 = h_a.shape
    C, D = CHUNK, HEAD_DIM
    W = nh * D
    pad = LANES - nh
    al = jnp.pad(a_log, (0, pad)).reshape(1, LANES)
    dtb = jnp.pad(dt_bias, (0, pad)).reshape(1, LANES)
    return pl.pallas_call(
        functools.partial(_mixa_prompt_kernel, nb=nb, nh=nh),
        grid=(t // C,),
        in_specs=[pl.BlockSpec((nb, C, wa4), lambda c: (0, c, 0)),
                  pl.BlockSpec((nb, C, LANES), lambda c: (0, c, 0)),
                  pl.BlockSpec((CONV_W, 3 * W), lambda c: (0, 0)),
                  pl.BlockSpec((1, LANES), lambda c: (0, 0)),
                  pl.BlockSpec((1, LANES), lambda c: (0, 0)),
                  pl.BlockSpec((1, D), lambda c: (0, 0))],
        out_specs=[pl.BlockSpec((nb, C, W), lambda c: (0, c, 0)),
                   pl.BlockSpec((nb, nh, D, D), lambda c: (0, 0, 0, 0))],
        out_shape=[jax.ShapeDtypeStruct((nb, t, W), F32),
                   jax.ShapeDtypeStruct((nb, nh, D, D), F32)],
        scratch_shapes=[pltpu.VMEM((nb * nh, D, D), F32),
                        pltpu.VMEM((nb, 8, 3 * W), F32)],
        compiler_params=_cparams(("arbitrary",)),
        name="mixa_prompt",
    )(h_a, h_ab, cw, al, dtb, norm_w.reshape(1, D))


def _hgrn_prompt_kernel(h_ref, loglb_ref, log1mlb_ref, omlb_ref, nw_ref,
                        o_ref, sc_ref, st_scr, *, nb, nh):
    c = pl.program_id(0)
    C, D = CHUNK, HEAD_DIM
    W = nh * D
    G = nb * nh

    @pl.when(c == 0)
    def _():
        st_scr[...] = jnp.zeros_like(st_scr)

    x = h_ref[...]
    qx = _heads(x[:, :, 0:W], nh)
    fx = _heads(x[:, :, W:2 * W], nh)
    iv = _heads(x[:, :, 2 * W:3 * W], nh)
    zh = _heads(x[:, :, 3 * W:4 * W], nh)

    la = loglb_ref[...]
    lbb = log1mlb_ref[...] - _softplus(-fx)
    mx = jnp.maximum(la, lbb)
    logf = mx + jnp.log1p(jnp.exp(-jnp.abs(la - lbb)))
    kc = omlb_ref[...] * _sigmoid(-fx)
    q = _silu(qx)

    ii = lax.broadcasted_iota(jnp.int32, (C, C), 0)
    jj = lax.broadcasted_iota(jnp.int32, (C, C), 1)
    ri = lax.broadcasted_iota(jnp.int32, (C, D), 0)
    gam = _es_mask('gij,gjk->gik', jnp.broadcast_to((ii >= jj).astype(F32), (G, C, C)), logf)

    def block_starts(hb):
        g4 = gam.reshape(G, C // hb, hb, D)
        starts = g4[:, :, 0:1, :]
        nxt = jnp.concatenate([starts[:, 1:], starts[:, -1:]], axis=1)
        own = jnp.broadcast_to(starts, g4.shape).reshape(G, C, D)
        nx = jnp.broadcast_to(nxt, g4.shape).reshape(G, C, D)
        return own, nx

    attn = jnp.zeros((G, C, C), F32)
    hb = C // 2
    while hb >= 8:
        s = 2 * hb
        own, nx = block_starts(hb)
        upper = (ri % s) >= hb
        qs = jnp.where(upper, q * jnp.exp(gam - own), 0.0)
        ks = jnp.where(upper, 0.0, kc * jnp.exp(jnp.where(upper, 0.0, nx - gam)))
        attn = attn + jnp.where(ii // s == jj // s, _es('gik,gjk->gij', qs, ks), 0.0)
        hb //= 2
    own, _ = block_starts(8)
    qd = q * jnp.exp(gam - own)
    kd = kc * jnp.exp(own - gam)
    attn = attn + jnp.where((ii // 8 == jj // 8) & (ii >= jj), _es('gik,gjk->gij', qd, kd), 0.0)

    st = st_scr[...]
    o = _es('gck,gvk->gcv', q * jnp.exp(gam), st) + _es('gij,gjv->giv', attn, iv)
    glast = gam[:, C - 1:C, :]
    kdec = kc * jnp.exp(glast - gam)
    st_new = st * jnp.exp(glast) + _es('gcv,gck->gvk', iv, kdec)
    st_scr[...] = st_new

    o = _gated_rms(o, nw_ref[...], zh)
    o_ref[...] = _unheads(o, nb, nh)

    @pl.when(c == pl.num_programs(0) - 1)
    def _():
        di = lax.broadcasted_iota(jnp.int32, (D, D), 0)
        dj = lax.broadcasted_iota(jnp.int32, (D, D), 1)
        eye = jnp.broadcast_to((di == dj).astype(F32), (G, D, D))
        sc_ref[...] = _es_mask('gkj,gvj->gkv', eye, st_new).reshape(nb, nh, D, D)


def _hgrn_prompt(h_c, lb, norm_w, nh):
    nb, t, wc4 = h_c.shape
    C, D = CHUNK, HEAD_DIM
    W = nh * D
    G = nb * nh

    def per_group(a):
        return jnp.tile(a.reshape(nh, 1, D), (nb, 1, 1))

    return pl.pallas_call(
        functools.partial(_hgrn_prompt_kernel, nb=nb, nh=nh),
        grid=(t // C,),
        in_specs=[pl.BlockSpec((nb, C, wc4), lambda c: (0, c, 0)),
                  pl.BlockSpec((G, 1, D), lambda c: (0, 0, 0)),
                  pl.BlockSpec((G, 1, D), lambda c: (0, 0, 0)),
                  pl.BlockSpec((G, 1, D), lambda c: (0, 0, 0)),
                  pl.BlockSpec((1, D), lambda c: (0, 0))],
        out_specs=[pl.BlockSpec((nb, C, W), lambda c: (0, c, 0)),
                   pl.BlockSpec((nb, nh, D, D), lambda c: (0, 0, 0, 0))],
        out_shape=[jax.ShapeDtypeStruct((nb, t, W), F32),
                   jax.ShapeDtypeStruct((nb, nh, D, D), F32)],
        scratch_shapes=[pltpu.VMEM((G, D, D), F32)],
        compiler_params=_cparams(("arbitrary",)),
        name="hgrn_prompt",
    )(h_c, per_group(jnp.log(lb)), per_group(jnp.log1p(-lb)), per_group(1.0 - lb),
      norm_w.reshape(1, D))


def _mixa_sample_kernel(al_ref, dtb_ref, xq_ref, xk_ref, xv_ref, z_ref, ab_ref,
                        cwq_ref, cwk_ref, cwv_ref, nw_ref, s0_ref,
                        o_ref, s_ref, kq_scr, *, nh, ts):
    h = pl.program_id(0)
    D = HEAD_DIM
    s_ref[...] = s0_ref[...]
    neg_a = -jnp.exp(al_ref[h])
    dtb = dtb_ref[h]

    def conv(x_ref, cw_ref, t):
        acc = x_ref[t] * cw_ref[0]
        for j in range(1, CONV_W):
            acc = acc + x_ref[t + j] * cw_ref[j]
        return _silu(acc)

    for t in range(ts):
        q = conv(xq_ref, cwq_ref, t)
        k = conv(xk_ref, cwk_ref, t)
        v = conv(xv_ref, cwv_ref, t)
        q = q * lax.rsqrt(jnp.sum(q * q, 0, keepdims=True) + 1e-6) * (D ** -0.5)
        k = k * lax.rsqrt(jnp.sum(k * k, 0, keepdims=True) + 1e-6)
        a = ab_ref[t, pl.ds(h, 1), :]
        b = ab_ref[t, pl.ds(nh + h, 1), :]
        decay = jnp.exp(neg_a * _softplus(a + dtb))
        beta = _sigmoid(b)
        kq_scr[0] = k
        kq_scr[1] = q

        def pass1(d, r):
            kd = kq_scr[0, pl.ds(d, 1), :]
            return r + (s_ref[0, d] * decay) * kd

        r = lax.fori_loop(0, D, pass1, jnp.zeros_like(v))
        vn = beta * (v - r)

        def pass2(d, o):
            kd = kq_scr[0, pl.ds(d, 1), :]
            qd = kq_scr[1, pl.ds(d, 1), :]
            sd = s_ref[0, d] * decay + kd * vn
            s_ref[0, d] = sd
            return o + sd * qd

        o = lax.fori_loop(0, D, pass2, jnp.zeros_like(v))
        o = o * lax.rsqrt(jnp.mean(o * o, 0, keepdims=True) + RMS_EPS)
        o_ref[t] = o * nw_ref[...] * _silu(z_ref[t])


def _mixa_sample(x_t, z_t, ab_t, cw_t, a_log, dt_bias, norm_w, s0_t, nh, ts):
    D = HEAD_DIM
    W = nh * D
    nb = x_t.shape[-1]
    tt = x_t.shape[0]
    nab = ab_t.shape[1]
    smem = pl.BlockSpec(memory_space=pltpu.SMEM)
    return pl.pallas_call(
        functools.partial(_mixa_sample_kernel, nh=nh, ts=ts),
        grid=(nh,),
        in_specs=[smem, smem,
                  pl.BlockSpec((tt, D, nb), lambda h: (0, h, 0)),
                  pl.BlockSpec((tt, D, nb), lambda h: (0, nh + h, 0)),
                  pl.BlockSpec((tt, D, nb), lambda h: (0, 2 * nh + h, 0)),
                  pl.BlockSpec((ts, D, nb), lambda h: (0, h, 0)),
                  pl.BlockSpec((ts, nab, nb), lambda h: (0, 0, 0)),
                  pl.BlockSpec((CONV_W, D, 1), lambda h: (0, h, 0)),
                  pl.BlockSpec((CONV_W, D, 1), lambda h: (0, nh + h, 0)),
                  pl.BlockSpec((CONV_W, D, 1), lambda h: (0, 2 * nh + h, 0)),
                  pl.BlockSpec((D, 1), lambda h: (0, 0)),
                  pl.BlockSpec((1, D, D, nb), lambda h: (h, 0, 0, 0))],
        out_specs=[pl.BlockSpec((ts, D, nb), lambda h: (0, h, 0)),
                   pl.BlockSpec((1, D, D, nb), lambda h: (h, 0, 0, 0))],
        out_shape=[jax.ShapeDtypeStruct((ts, W, nb), F32),
                   jax.ShapeDtypeStruct((nh, D, D, nb), F32)],
        scratch_shapes=[pltpu.VMEM((2, D, nb), F32)],
        compiler_params=_cparams(("arbitrary",)),
        name="mixa_sample",
    )(a_log, dt_bias, x_t, x_t, x_t, z_t, ab_t, cw_t, cw_t, cw_t,
      norm_w.reshape(D, 1), s0_t)


def _hgrn_sample_kernel(q_ref, f_ref, i_ref, z_ref, lb_ref, nw_ref, s0_ref,
                        o_ref, s_ref, kq_scr, *, ts):
    D = HEAD_DIM
    s_ref[...] = s0_ref[...]
    lb = lb_ref[...]
    for t in range(ts):
        fx = f_ref[t]
        f = lb + (1.0 - lb) * _sigmoid(fx)
        kq_scr[0] = f
        kq_scr[1] = (1.0 - lb) * _sigmoid(-fx)
        kq_scr[2] = _silu(q_ref[t])
        v = i_ref[t]

        def body(d, o):
            fd = kq_scr[0, pl.ds(d, 1), :]
            kd = kq_scr[1, pl.ds(d, 1), :]
            qd = kq_scr[2, pl.ds(d, 1), :]
            sd = s_ref[0, d] * fd + kd * v
            s_ref[0, d] = sd
            return o + sd * qd

        o = lax.fori_loop(0, D, body, jnp.zeros_like(v))
        o = o * lax.rsqrt(jnp.mean(o * o, 0, keepdims=True) + RMS_EPS)
        o_ref[t] = o * nw_ref[...] * _silu(z_ref[t])


def _hgrn_sample(hc_t, lb, norm_w, s0_t, nh, ts):
    D = HEAD_DIM
    W = nh * D
    nb = hc_t.shape[-1]
    return pl.pallas_call(
        functools.partial(_hgrn_sample_kernel, ts=ts),
        grid=(nh,),
        in_specs=[pl.BlockSpec((ts, D, nb), lambda h: (0, h, 0)),
                  pl.BlockSpec((ts, D, nb), lambda h: (0, nh + h, 0)),
                  pl.BlockSpec((ts, D, nb), lambda h: (0, 2 * nh + h, 0)),
                  pl.BlockSpec((ts, D, nb), lambda h: (0, 3 * nh + h, 0)),
                  pl.BlockSpec((D, 1), lambda h: (h, 0)),
                  pl.BlockSpec((D, 1), lambda h: (0, 0)),
                  pl.BlockSpec((1, D, D, nb), lambda h: (h, 0, 0, 0))],
        out_specs=[pl.BlockSpec((ts, D, nb), lambda h: (0, h, 0)),
                   pl.BlockSpec((1, D, D, nb), lambda h: (h, 0, 0, 0))],
        out_shape=[jax.ShapeDtypeStruct((ts, W, nb), F32),
                   jax.ShapeDtypeStruct((nh, D, D, nb), F32)],
        scratch_shapes=[pltpu.VMEM((3, D, nb), F32)],
        compiler_params=_cparams(("arbitrary",)),
        name="hgrn_sample",
    )(hc_t, hc_t, hc_t, hc_t, lb.reshape(W, 1), norm_w.reshape(D, 1), s0_t)


def _block_mean_kernel(k_ref, o_ref):
    o_ref[...] = jnp.mean(k_ref[...], axis=0, keepdims=True)


def _block_mean(k):
    nb, t, w = k.shape
    nblk = t // MOBA_BLOCK
    out = pl.pallas_call(
        _block_mean_kernel,
        grid=(nb, nblk),
        in_specs=[pl.BlockSpec((None, MOBA_BLOCK, w), lambda b, n: (b, n, 0))],
        out_specs=pl.BlockSpec((None, None, 1, w), lambda b, n: (b, n, 0, 0)),
        out_shape=jax.ShapeDtypeStruct((nb, nblk, 1, w), F32),
        compiler_params=_cparams(("parallel", "parallel")),
        name="moba_block_mean",
    )(k)
    return out.reshape(nb, nblk, w)


def _alibi_slope(h, nh):
    return float(2.0 ** (-8.0 * (h + 1) / nh))


LOG2E = 1.4426950408889634


def _moba_prompt_kernel(qi_ref, kj_ref, q_ref, k_ref, v_ref, km_ref, o_ref,
                        acc_scr, m_scr, l_scr, sel_scr, qs_scr, bias_scr, *, nh, nblk):
    step = pl.program_id(1)
    qi = qi_ref[step]
    kj = kj_ref[step]
    BS, D = MOBA_BLOCK, HEAD_DIM
    kpos = lax.broadcasted_iota(jnp.int32, (BS, BS), 0)
    qpos = lax.broadcasted_iota(jnp.int32, (BS, BS), 1)

    @pl.when(step == 0)
    def _():
        rel0 = (qpos - kpos).astype(F32)
        for h in range(nh):
            bias_scr[h] = (_alibi_slope(h, nh) * LOG2E) * rel0

    @pl.when(kj == 0)
    def _():
        m_scr[...] = jnp.full_like(m_scr, -jnp.inf)
        l_scr[...] = jnp.zeros_like(l_scr)
        acc_scr[...] = jnp.zeros_like(acc_scr)
        nn = lax.broadcasted_iota(jnp.int32, (nblk, BS), 0)
        elig = nn < qi
        for h in range(nh):
            qh = q_ref[:, h * D:(h + 1) * D]
            qs_scr[h] = (qh * (D ** -0.5 * LOG2E)).astype(BF16)
            gate = _es('nd,qd->nq', km_ref[:, h * D:(h + 1) * D], qh)
            gate = jnp.where(elig, gate, -jnp.inf)
            rank = jnp.zeros((nblk, BS), F32)
            for m in range(nblk):
                gm = gate[m:m + 1, :]
                ahead = (gm > gate) | ((gm == gate) & (nn > m))
                rank = rank + jnp.where(ahead, 1.0, 0.0)
            sel_scr[h] = jnp.where(elig & (rank < MOBA_TOPK), 0.0, -jnp.inf)

    def attend(mask_fn):
        hs = range(nh)
        raw = [jnp.einsum('kd,qd->kq', k_ref[:, h * D:(h + 1) * D].astype(BF16), qs_scr[h],
                          preferred_element_type=F32) for h in hs]
        st = [mask_fn(h, raw[h] - bias_scr[h]) for h in hs]
        m_old = m_scr[...]
        l_old = l_scr[...]
        m_new = jnp.concatenate([jnp.max(st[h], axis=0, keepdims=True) for h in hs], axis=0)
        m_new = jnp.maximum(m_old, m_new)
        m_safe = jnp.where(m_new == -jnp.inf, 0.0, m_new)
        alpha = jnp.exp2(m_old - m_safe)
        p = [jnp.exp2(st[h] - m_safe[h:h + 1, :]) for h in hs]
        l_scr[...] = alpha * l_old + jnp.concatenate(
            [jnp.sum(p[h], axis=0, keepdims=True) for h in hs], axis=0)
        m_scr[...] = m_new
        pv = [jnp.einsum('kd,kq->dq', v_ref[:, h * D:(h + 1) * D].astype(BF16), p[h].astype(BF16),
                         preferred_element_type=F32) for h in hs]
        for h in hs:
            acc_scr[h] = alpha[h:h + 1, :] * acc_scr[h] + pv[h]

    @pl.when(kj < qi)
    def _():
        off = ((qi - kj) * BS).astype(F32)
        attend(lambda h, s: s + (sel_scr[h, pl.ds(kj, 1), :] - (_alibi_slope(h, nh) * LOG2E) * off))

    @pl.when(kj == qi)
    def _():
        attend(lambda h, s: jnp.where(qpos >= kpos, s, -jnp.inf))
        outs = [acc_scr[h] / l_scr[h:h + 1, :] for h in range(nh)]
        o_ref[...] = jnp.concatenate(outs, axis=0).T


def _moba_prompt(q, k, v, nh):
    nb, t, w = q.shape
    BS, D = MOBA_BLOCK, HEAD_DIM
    nblk = t // BS
    km = _block_mean(k)
    pairs = [(i, j) for i in range(nblk) for j in range(i + 1)]
    qi = jnp.asarray([p[0] for p in pairs], jnp.int32)
    kj = jnp.asarray([p[1] for p in pairs], jnp.int32)
    grid_spec = pltpu.PrefetchScalarGridSpec(
        num_scalar_prefetch=2,
        grid=(nb, len(pairs)),
        in_specs=[pl.BlockSpec((None, BS, w), lambda b, s, qi, kj: (b, qi[s], 0)),
                  pl.BlockSpec((None, BS, w), lambda b, s, qi, kj: (b, kj[s], 0)),
                  pl.BlockSpec((None, BS, w), lambda b, s, qi, kj: (b, kj[s], 0)),
                  pl.BlockSpec((None, nblk, w), lambda b, s, qi, kj: (b, 0, 0))],
        out_specs=pl.BlockSpec((None, BS, w), lambda b, s, qi, kj: (b, qi[s], 0)),
        scratch_shapes=[pltpu.VMEM((nh, D, BS), F32),
                        pltpu.VMEM((nh, BS), F32),
                        pltpu.VMEM((nh, BS), F32),
                        pltpu.VMEM((nh, nblk, BS), F32),
                        pltpu.VMEM((nh, BS, D), BF16),
                        pltpu.VMEM((nh, BS, BS), F32)])
    return pl.pallas_call(
        functools.partial(_moba_prompt_kernel, nh=nh, nblk=nblk),
        grid_spec=grid_spec,
        out_shape=jax.ShapeDtypeStruct((nb, t, w), F32),
        compiler_params=_cparams(("parallel", "arbitrary")),
        name="moba_prompt",
    )(qi, kj, q, k, v, km)


def _moba_sample_kernel(pt_ref, q_ref, kn_ref, vn_ref, *refs, nh, ts, past_len, bps):
    npg = 2 * bps
    k_refs, v_refs = refs[:npg], refs[npg:2 * npg]
    o_ref, m_scr, l_scr, g_scr, acc_scr = refs[2 * npg:]
    step = pl.program_id(1)
    BS, D = MOBA_BLOCK, HEAD_DIM
    page = BS // 2
    TP = q_ref.shape[1]
    scale = D ** -0.5
    q = q_ref[...]
    qb = q.astype(BF16)

    tpos = lax.broadcasted_iota(jnp.int32, (nh, TP, 1), 1).astype(F32)
    hidx = lax.broadcasted_iota(jnp.int32, (nh, TP, 1), 0)
    slope = jnp.zeros((nh, TP, 1), F32)
    for h in range(nh):
        slope = jnp.where(hidx == h, _alibi_slope(h, nh), slope)
    kofs = lax.broadcasted_iota(jnp.int32, (nh, TP, page), 2).astype(F32)
    qpos = past_len + tpos

    pages = range(npg)
    raw = [jnp.einsum('htd,hdk->htk', qb, k_refs[i][...].astype(BF16), preferred_element_type=F32)
           for i in pages]
    scores = []
    for i in pages:
        kpos = ((step * bps + i // 2) * BS + (i % 2) * page).astype(F32) + kofs
        scores.append(raw[i] * scale - slope * (qpos - kpos))
    m_blk = [jnp.maximum(jnp.max(scores[2 * b], axis=-1, keepdims=True),
                         jnp.max(scores[2 * b + 1], axis=-1, keepdims=True)) for b in range(bps)]
    p = [jnp.exp(scores[i] - m_blk[i // 2]) for i in pages]
    pv = [jnp.einsum('htk,hdk->htd', p[i].astype(BF16), v_refs[i][...].astype(BF16),
                     preferred_element_type=F32) for i in pages]
    for b in range(bps):
        n = step * bps + b
        l_n = (jnp.sum(p[2 * b], axis=-1, keepdims=True)
               + jnp.sum(p[2 * b + 1], axis=-1, keepdims=True))
        gate = (jnp.sum(raw[2 * b], axis=-1, keepdims=True)
                + jnp.sum(raw[2 * b + 1], axis=-1, keepdims=True)) * (1.0 / BS)
        m_scr[n] = jnp.broadcast_to(m_blk[b], (nh, TP, LANES))
        l_scr[n] = jnp.broadcast_to(l_n, (nh, TP, LANES))
        g_scr[n] = jnp.broadcast_to(gate, (nh, TP, LANES))
        acc_scr[n] = pv[2 * b] + pv[2 * b + 1]

    @pl.when(step == pl.num_programs(1) - 1)
    def _():
        nb_past = past_len // BS
        gates = [g_scr[i][:, :, 0:1] for i in range(nb_past)]
        sels = []
        for i in range(nb_past):
            rank = jnp.zeros((nh, TP, 1), F32)
            for m in range(nb_past):
                if m == i:
                    continue
                ahead = (gates[m] > gates[i]) if m > i else (gates[m] >= gates[i])
                rank = rank + jnp.where(ahead, 1.0, 0.0)
            sels.append(rank < MOBA_TOPK)
        kn = kn_ref[...]
        vn = vn_ref[...]
        s_own = []
        for j in range(ts):
            sj = jnp.sum(q * kn[:, j:j + 1, :], axis=-1, keepdims=True) * scale
            sj = sj - slope * (tpos - j)
            s_own.append(jnp.where(tpos >= j, sj, -jnp.inf))
        m_tot = s_own[0]
        for j in range(1, ts):
            m_tot = jnp.maximum(m_tot, s_own[j])
        for i in range(nb_past):
            m_tot = jnp.maximum(m_tot, jnp.where(sels[i], m_scr[i][:, :, 0:1], -jnp.inf))
        num = jnp.zeros((nh, TP, D), F32)
        den = jnp.zeros((nh, TP, 1), F32)
        for j in range(ts):
            pj = jnp.exp(s_own[j] - m_tot)
            num = num + pj * vn[:, j:j + 1, :]
            den = den + pj
        for i in range(nb_past):
            wi = jnp.where(sels[i], jnp.exp(m_scr[i][:, :, 0:1] - m_tot), 0.0)
            num = num + wi * acc_scr[i]
            den = den + wi * l_scr[i][:, :, 0:1]
        o_ref[...] = num / den


def _moba_sample(q, kn, vn, cache_kt, cache_vt, page_table, layer, ts):
    nb, nh, TP, D = q.shape
    page = cache_kt.shape[4]
    n_pages = page_table.shape[1]
    past_len = n_pages * page
    BS = MOBA_BLOCK
    assert BS == 2 * page and past_len % BS == 0 and ts <= TP
    nblk = past_len // BS
    bps = next(c for c in (4, 2, 1) if nblk % c == 0)
    npg = 2 * bps
    pt = page_table.reshape(-1)

    def page_spec(i):
        return pl.BlockSpec((None, None, nh, D, page),
                            lambda b, s, pt: (layer, pt[b * n_pages + s * npg + i], 0, 0, 0))

    tok = pl.BlockSpec((None, nh, TP, D), lambda b, s, pt: (b, 0, 0, 0))
    pages = [page_spec(i) for i in range(npg)]
    grid_spec = pltpu.PrefetchScalarGridSpec(
        num_scalar_prefetch=1,
        grid=(nb, nblk // bps),
        in_specs=[tok, tok, tok] + pages + pages,
        out_specs=tok,
        scratch_shapes=[pltpu.VMEM((nblk, nh, TP, LANES), F32),
                        pltpu.VMEM((nblk, nh, TP, LANES), F32),
                        pltpu.VMEM((nblk, nh, TP, LANES), F32),
                        pltpu.VMEM((nblk, nh, TP, D), F32)])
    return pl.pallas_call(
        functools.partial(_moba_sample_kernel, nh=nh, ts=ts, past_len=past_len, bps=bps),
        grid_spec=grid_spec,
        out_shape=jax.ShapeDtypeStruct((nb, nh, TP, D), F32),
        compiler_params=_cparams(("parallel", "arbitrary")),
        name="moba_sample",
    )(pt, q, kn, vn, *([cache_kt] * npg), *([cache_vt] * npg))


def _router_gates(y, r, n_exp):
    logits = _es('mk,kn->mn', y, r)
    lane = lax.broadcasted_iota(jnp.int32, logits.shape, 1)
    logits = jnp.where(lane < n_exp, logits, -jnp.inf)
    m1 = jnp.max(logits, -1, keepdims=True)
    i1 = jnp.min(jnp.where(logits == m1, lane, LANES), -1, keepdims=True)
    first = lane == i1
    rest = jnp.where(first, -jnp.inf, logits)
    m2 = jnp.max(rest, -1, keepdims=True)
    i2 = jnp.min(jnp.where(rest == m2, lane, LANES), -1, keepdims=True)
    second = lane == i2
    e = jnp.exp(m2 - m1)
    den = 1.0 + e
    return jnp.where(first, 1.0 / den, 0.0) + jnp.where(second, e / den, 0.0)


def _out_proj_kernel(*refs, alpha, wa, wb, n_exp):
    if n_exp:
        x_ref, oa_ref, ob_ref, oc_ref, w_ref, g_ref, b_ref, r_ref, o_ref, gate_ref = refs
    else:
        x_ref, oa_ref, ob_ref, oc_ref, w_ref, g_ref, b_ref, o_ref = refs
    acc = jnp.dot(oa_ref[...].astype(BF16), w_ref[0:wa, :], preferred_element_type=F32)
    acc = acc + jnp.dot(ob_ref[...].astype(BF16), w_ref[wa:wa + wb, :], preferred_element_type=F32)
    acc = acc + jnp.dot(oc_ref[...].astype(BF16), w_ref[wa + wb:, :], preferred_element_type=F32)
    y = _layer_norm(alpha * x_ref[...] + acc, g_ref[...], b_ref[...])
    o_ref[...] = y
    if n_exp:
        gate_ref[...] = _router_gates(y, r_ref[...], n_exp)


def _out_proj(x, oa, ob, oc, w, g, b, router, alpha, tm):
    n, d = x.shape
    wa, wb, wc = oa.shape[1], ob.shape[1], oc.shape[1]
    n_exp = 0 if router is None else router.shape[1]
    row = lambda width: pl.BlockSpec((tm, width), lambda i: (i, 0))
    full = lambda shape: pl.BlockSpec(shape, lambda i: (0, 0))
    in_specs = [row(d), row(wa), row(wb), row(wc), full(w.shape), full((1, d)), full((1, d))]
    args = [x, oa, ob, oc, w, g.reshape(1, d), b.reshape(1, d)]
    out_specs = [row(d)]
    out_shape = [jax.ShapeDtypeStruct((n, d), F32)]
    if n_exp:
        in_specs.append(full((d, LANES)))
        args.append(jnp.pad(router, ((0, 0), (0, LANES - n_exp))))
        out_specs.append(row(LANES))
        out_shape.append(jax.ShapeDtypeStruct((n, LANES), F32))
    res = pl.pallas_call(
        functools.partial(_out_proj_kernel, alpha=alpha, wa=wa, wb=wb, n_exp=n_exp),
        grid=(n // tm,),
        in_specs=in_specs, out_specs=out_specs, out_shape=out_shape,
        compiler_params=_cparams(("parallel",)),
        name="out_proj_ln",
    )(*args)
    return (res[0], res[1]) if n_exp else (res[0], None)


def _ffn_kernel(x_ref, w1_ref, w3_ref, w2_ref, g_ref, b_ref, o_ref, acc_scr, xb_scr, *, alpha):
    f = pl.program_id(1)

    @pl.when(f == 0)
    def _():
        acc_scr[...] = jnp.zeros_like(acc_scr)
        xb_scr[...] = x_ref[...].astype(BF16)

    xb = xb_scr[...]
    h1 = jnp.dot(xb, w1_ref[...].astype(BF16), preferred_element_type=F32)
    h3 = jnp.dot(xb, w3_ref[...].astype(BF16), preferred_element_type=F32)
    hid = (_silu(h1) * h3).astype(BF16)
    acc_scr[...] += jnp.dot(hid, w2_ref[...].astype(BF16), preferred_element_type=F32)

    @pl.when(f == pl.num_programs(1) - 1)
    def _():
        o_ref[...] = _layer_norm(alpha * x_ref[...] + acc_scr[...], g_ref[...], b_ref[...])


def _ffn_dense(x, w1, w3, w2, g, b, alpha, tm, tf):
    n, d = x.shape
    dff = w1.shape[1]
    return pl.pallas_call(
        functools.partial(_ffn_kernel, alpha=alpha),
        grid=(n // tm, dff // tf),
        in_specs=[pl.BlockSpec((tm, d), lambda i, f: (i, 0)),
                  pl.BlockSpec((d, tf), lambda i, f: (0, f)),
                  pl.BlockSpec((d, tf), lambda i, f: (0, f)),
                  pl.BlockSpec((tf, d), lambda i, f: (f, 0)),
                  pl.BlockSpec((1, d), lambda i, f: (0, 0)),
                  pl.BlockSpec((1, d), lambda i, f: (0, 0))],
        out_specs=pl.BlockSpec((tm, d), lambda i, f: (i, 0)),
        out_shape=jax.ShapeDtypeStruct((n, d), F32),
        scratch_shapes=[pltpu.VMEM((tm, d), F32), pltpu.VMEM((tm, d), BF16)],
        compiler_params=_cparams(("parallel", "arbitrary")),
        name="ffn_dense",
    )(x, w1, w3, w2, g.reshape(1, d), b.reshape(1, d))


def _moe_kernel(x_ref, gate_ref, w1_ref, w3_ref, w2_ref, g_ref, b_ref, o_ref,
                acc_scr, xb_scr, xc_scr, yc_scr, posc_scr, post_scr, selt_scr, cnt_smem,
                *, alpha, ch, nch):
    e = pl.program_id(1)
    f = pl.program_id(2)
    first_f = f == 0
    last_f = f == pl.num_programs(2) - 1
    tm = x_ref.shape[0]

    @pl.when((e == 0) & first_f)
    def _():
        acc_scr[...] = jnp.zeros_like(acc_scr)
        xb_scr[...] = x_ref[...].astype(BF16)
        sel = jnp.where(gate_ref[...] > 0.0, 1.0, 0.0)
        ri = lax.broadcasted_iota(jnp.int32, (tm, tm), 0)
        ci = lax.broadcasted_iota(jnp.int32, (tm, tm), 1)
        before = jnp.where(ci < ri, 1.0, 0.0).astype(BF16)
        pos = jnp.dot(before, sel.astype(BF16), preferred_element_type=F32)
        posc_scr[...] = pos
        post_scr[...] = pos.T
        selt_scr[...] = sel.T

    @pl.when(first_f)
    def _():
        lane = lax.broadcasted_iota(jnp.int32, (tm, LANES), 1)
        cnt = jnp.sum(jnp.where((lane == e) & (gate_ref[...] > 0.0), 1.0, 0.0))
        cnt_smem[0] = cnt.astype(jnp.int32)
        prow = post_scr[pl.ds(e, 1), :]
        srow = selt_scr[pl.ds(e, 1), :]
        for k in range(nch):
            @pl.when(cnt_smem[0] > k * ch)
            def _():
                r = (lax.broadcasted_iota(jnp.int32, (ch, tm), 0) + k * ch).astype(F32)
                pk = jnp.where((prow == r) & (srow > 0.0), 1.0, 0.0).astype(BF16)
                xc_scr[k] = jnp.dot(pk, xb_scr[...], preferred_element_type=F32).astype(BF16)

    cnt = cnt_smem[0]
    for k in range(nch):
        @pl.when(cnt > k * ch)
        def _():
            xk = xc_scr[k]
            h1 = jnp.dot(xk, w1_ref[...].astype(BF16), preferred_element_type=F32)
            h3 = jnp.dot(xk, w3_ref[...].astype(BF16), preferred_element_type=F32)
            hid = (_silu(h1) * h3).astype(BF16)
            part = jnp.dot(hid, w2_ref[...].astype(BF16), preferred_element_type=F32)

            @pl.when(first_f)
            def _():
                yc_scr[k] = part

            @pl.when(jnp.logical_not(first_f))
            def _():
                yc_scr[k] += part

    @pl.when(last_f)
    def _():
        lane = lax.broadcasted_iota(jnp.int32, (tm, LANES), 1)
        mine = lane == e
        gcol = jnp.sum(jnp.where(mine, gate_ref[...], 0.0), -1, keepdims=True)
        pcol = jnp.sum(jnp.where(mine, posc_scr[...], 0.0), -1, keepdims=True)
        for k in range(nch):
            @pl.when(cnt > k * ch)
            def _():
                r = (lax.broadcasted_iota(jnp.int32, (tm, ch), 1) + k * ch).astype(F32)
                ptk = jnp.where((pcol == r) & (gcol > 0.0), 1.0, 0.0).astype(BF16)
                acc_scr[...] += gcol * jnp.dot(ptk, yc_scr[k].astype(BF16),
                                               preferred_element_type=F32)

    @pl.when((e == pl.num_programs(1) - 1) & last_f)
    def _():
        o_ref[...] = _layer_norm(alpha * x_ref[...] + acc_scr[...], g_ref[...], b_ref[...])


def _moe_weight_tiles(w1, w3, w2, tf):
    n_exp, d, dff = w1.shape

    def tiles(w):
        return w.astype(BF16).reshape(n_exp, d, dff // tf, tf).transpose(0, 2, 1, 3)

    return tiles(w1), tiles(w3), w2.astype(BF16)


def _ffn_moe(x, gate, w1, w3, w2, g, b, alpha, tm):
    n, d = x.shape
    n_exp, nf, _, tf = w1.shape
    dff = nf * tf
    ch = min(tm, -(-(tm * MOE_TOPK * 3) // (n_exp * 2 * LANES)) * LANES)
    nch = -(-tm // ch)
    return pl.pallas_call(
        functools.partial(_moe_kernel, alpha=alpha, ch=ch, nch=nch),
        grid=(n // tm, n_exp, dff // tf),
        in_specs=[pl.BlockSpec((tm, d), lambda i, e, f: (i, 0), pipeline_mode=pl.Buffered(1)),
                  pl.BlockSpec((tm, LANES), lambda i, e, f: (i, 0)),
                  pl.BlockSpec((None, None, d, tf), lambda i, e, f: (e, f, 0, 0)),
                  pl.BlockSpec((None, None, d, tf), lambda i, e, f: (e, f, 0, 0)),
                  pl.BlockSpec((None, tf, d), lambda i, e, f: (e, f, 0)),
                  pl.BlockSpec((1, d), lambda i, e, f: (0, 0)),
                  pl.BlockSpec((1, d), lambda i, e, f: (0, 0))],
        out_specs=pl.BlockSpec((tm, d), lambda i, e, f: (i, 0), pipeline_mode=pl.Buffered(1)),
        out_shape=jax.ShapeDtypeStruct((n, d), F32),
        scratch_shapes=[pltpu.VMEM((tm, d), F32), pltpu.VMEM((tm, d), BF16),
                        pltpu.VMEM((nch, ch, d), BF16), pltpu.VMEM((nch, ch, d), F32),
                        pltpu.VMEM((tm, LANES), F32), pltpu.VMEM((LANES, tm), F32),
                        pltpu.VMEM((LANES, tm), F32), pltpu.SMEM((1,), jnp.int32)],
        compiler_params=_cparams(("parallel", "arbitrary", "arbitrary")),
        name="ffn_moe",
    )(x, gate, w1, w3, w2, g.reshape(1, d), b.reshape(1, d))


def _row_tile(n, cap):
    t = cap
    while n % t:
        t //= 2
    return t


def _ff_tile(dff):
    for t in (512, 256, 128):
        if dff % t == 0:
            return t
    return dff


def kernel(x_prompt, x_sample, cache_k, cache_v, page_table, state_a, state_a_conv, state_c, ln_in_g, ln_in_b, w_in, w_out, a_conv_w, a_a_log, a_dt_bias, a_norm_w, c_lb_logits, c_norm_w, ln1_g, ln1_b, ln2_g, ln2_b, ffn_w1, ffn_w3, ffn_w2, moe_router, moe_w1, moe_w3, moe_w2):
    bp, tp, d = x_prompt.shape
    bs, ts, _ = x_sample.shape
    depth = w_in.shape[0]
    D = HEAD_DIM
    wa, wb = d // 4, d // 2
    wc = d - wa - wb
    ha, hb, hc = wa // D, wb // D, wc // D
    kc = hc * D
    alpha = (2.0 * depth) ** 0.25
    off_b = 4 * wa + 2 * ha
    off_c = off_b + 3 * wb
    p_in = off_c + 2 * kc + 2 * wc
    assert w_in.shape[2] == p_in and 2 * ha <= LANES

    seg_w = [4 * wa, wb, wb, wb, 2 * kc + 2 * wc, LANES]
    segs, lo = [], 0
    for sw in seg_w:
        segs.append((lo, lo + sw))
        lo += sw

    lb_all = jnp.cumsum(jax.nn.softmax(c_lb_logits.astype(F32), axis=0), axis=0)
    lb_all = lb_all - lb_all[0:1]

    n_pool, page = cache_k.shape[1], cache_k.shape[2]
    ck = cache_k.transpose(0, 1, 3, 4, 2)
    cv = cache_v.transpose(0, 1, 3, 4, 2)
    tp_pad = -(-ts // 8) * 8

    np_rows, ns_rows = bp * tp, bs * ts
    tm_p = _row_tile(np_rows, 256)
    tm_s = _row_tile(ns_rows, 256)
    xp = _ln_rows(x_prompt.reshape(np_rows, d), ln_in_g, ln_in_b, tm_p)
    xs = _ln_rows(x_sample.reshape(ns_rows, d), ln_in_g, ln_in_b, tm_s)

    outs = {k: [] for k in ("ks", "vs", "sap", "sas", "cvp", "cvs", "scp", "scs")}
    kv_t = None
    for l in range(depth):
        wl = w_in[l]
        w_perm = jnp.concatenate(
            [wl[:, :4 * wa], wl[:, off_b:], wl[:, 4 * wa:off_b],
             jnp.zeros((d, LANES - 2 * ha), wl.dtype)], axis=1).astype(BF16)
        w_o = w_out[l].astype(BF16)
        cw = a_conv_w[l]

        h_a, q_b, k_b, v_b, h_c, h_ab, *kv_t = _in_proj(
            xp, w_perm, segs, tm_p, kv_stack=(l, depth, bp, tp, hb, (2, 3), kv_t))
        h_a3 = h_a.reshape(bp, tp, 4 * wa)
        oa, sa_new = _mixa_prompt(h_a3, h_ab.reshape(bp, tp, LANES), cw, a_a_log[l], a_dt_bias[l],
                                  a_norm_w[l], ha)
        ob = _moba_prompt(q_b.reshape(bp, tp, wb), k_b.reshape(bp, tp, wb), v_b.reshape(bp, tp, wb), hb)
        oc, sc_new = _hgrn_prompt(h_c.reshape(bp, tp, 4 * wc), lb_all[l], c_norm_w[l], hc)
        outs["sap"].append(sa_new)
        conv_full = jnp.concatenate([jnp.zeros((bp, CONV_W - 1, 3 * wa), F32), h_a3[:, :, :3 * wa]], axis=1) \
            if tp < CONV_W - 1 else h_a3[:, :, :3 * wa]
        outs["cvp"].append(conv_full[:, conv_full.shape[1] - (CONV_W - 1):])
        outs["scp"].append(sc_new)
        router = moe_router[l // 2] if l % 2 else None
        x1p, gate_p = _out_proj(xp, oa.reshape(np_rows, wa), ob.reshape(np_rows, wb),
                                oc.reshape(np_rows, wc), w_o, ln1_g[l], ln1_b[l], router, alpha, tm_p)

        h_a, q_b, k_b, v_b, h_c, h_ab = _in_proj(xs, w_perm, segs, tm_s)
        h_a3 = h_a.reshape(bs, ts, 4 * wa)
        x_full = jnp.concatenate([state_a_conv[l].astype(F32), h_a3[:, :, :3 * wa]], axis=1)
        x_t = x_full.transpose(1, 2, 0)
        z_t = h_a3[:, :, 3 * wa:].transpose(1, 2, 0)
        ab_t = h_ab.reshape(bs, ts, LANES)[:, :, :max(8, 2 * ha)].transpose(1, 2, 0)
        s0_t = state_a[l].astype(F32).transpose(1, 2, 3, 0)
        oa_t, sa_t = _mixa_sample(x_t, z_t, ab_t, cw.reshape(CONV_W, 3 * wa, 1), a_a_log[l], a_dt_bias[l],
                                  a_norm_w[l], s0_t, ha, ts)
        oa = oa_t.transpose(2, 0, 1).reshape(ns_rows, wa)
        outs["sas"].append(sa_t.transpose(3, 0, 1, 2))
        outs["cvs"].append(x_full[:, x_full.shape[1] - (CONV_W - 1):])
        def head_major(a):
            a = a.reshape(bs, ts, hb, D).transpose(0, 2, 1, 3)
            return jnp.pad(a, ((0, 0), (0, 0), (0, tp_pad - ts), (0, 0)))

        ob = _moba_sample(head_major(q_b), head_major(k_b), head_major(v_b), ck, cv, page_table, l, ts)
        ob = ob[:, :, :ts].transpose(0, 2, 1, 3)
        outs["ks"].append(k_b.reshape(bs, ts, hb, D))
        outs["vs"].append(v_b.reshape(bs, ts, hb, D))
        hc_t = h_c.reshape(bs, ts, 4 * wc).transpose(1, 2, 0)
        sc0_t = state_c[l].astype(F32).transpose(1, 2, 3, 0)
        oc_t, sc_t = _hgrn_sample(hc_t, lb_all[l], c_norm_w[l], sc0_t, hc, ts)
        oc = oc_t.transpose(2, 0, 1).reshape(ns_rows, wc)
        outs["scs"].append(sc_t.transpose(3, 0, 1, 2))
        x1s, gate_s = _out_proj(xs, oa, ob.reshape(ns_rows, wb), oc, w_o, ln1_g[l], ln1_b[l], router,
                                alpha, tm_s)

        if l % 2 == 0:
            w1, w3, w2 = ffn_w1[l // 2].astype(BF16), ffn_w3[l // 2].astype(BF16), ffn_w2[l // 2].astype(BF16)
            tf = _ff_tile(w1.shape[1])
            xp = _ffn_dense(x1p, w1, w3, w2, ln2_g[l], ln2_b[l], alpha, _row_tile(np_rows, 1024), tf)
            xs = _ffn_dense(x1s, w1, w3, w2, ln2_g[l], ln2_b[l], alpha, _row_tile(ns_rows, 1024), tf)
        else:
            dff = moe_w1.shape[3]
            tf = dff // 2 if dff % (2 * LANES) == 0 else _ff_tile(dff)
            w1, w3, w2 = _moe_weight_tiles(moe_w1[l // 2], moe_w3[l // 2], moe_w2[l // 2], tf)
            xp = _ffn_moe(x1p, gate_p, w1, w3, w2, ln2_g[l], ln2_b[l], alpha, _row_tile(np_rows, 1024))
            xs = _ffn_moe(x1s, gate_s, w1, w3, w2, ln2_g[l], ln2_b[l], alpha, _row_tile(ns_rows, 1024))

    st = lambda key: jnp.stack(outs[key])
    k_prompt, v_prompt = (a.transpose(0, 1, 4, 2, 3) for a in kv_t)
    return (xp.reshape(bp, tp, d), xs.reshape(bs, ts, d), k_prompt, v_prompt, st("ks"), st("vs"),
            st("sap"), st("sas"), st("cvp"), st("cvs"), st("scp"), st("scs"))
```
